```python
import math
import jax, jax.numpy as jnp
from jax import lax
import numpy as np

D_MODEL = 1024
BATCH = 4
SEQ = 4096
DEPTH = 4

GRID_W = 64
CTX_LEN = 256

D_HYENA = 256
HYENA_ORDER = 2
HYENA_EMB = 33
HYENA_FILTER_WIDTH = 64
HYENA_TARGET = 1e-2
HYENA_MIN_DECAY = math.log(HYENA_TARGET) / 1.5
HYENA_MAX_DECAY = math.log(HYENA_TARGET) / 0.3
SHORT_CONV = 3
N_DIFF_HEADS = 8
DIFF_QK = 32
DIFF_V = 64
D_DIFF = N_DIFF_HEADS * DIFF_V
D_FNET = 256
FNET_GROUPS = 4
FNET_GROUP_W = D_FNET // FNET_GROUPS
D_MIX = D_HYENA + D_DIFF + D_FNET
ROPE_BASE = 10000.0
ATTN_BLOCK = 128

HY_W = 3 * D_HYENA
QK_W = N_DIFF_HEADS * 2 * DIFF_QK
Q_OFF = HY_W
K_OFF = Q_OFF + QK_W
V_OFF = K_OFF + QK_W
F_OFF = V_OFF + D_DIFF
D_IN = F_OFF + D_FNET

N_EXPERTS = 64
TOP_K = 8
N_GROUPS = 8
TOPK_GROUPS = 4
D_EXPERT = 256
D_SHARED = 256
ROUTED_SCALE = 2.5
MOE_BLOCK = 128

ALPHA = (2 * DEPTH) ** 0.25
BETA = (8 * DEPTH) ** -0.25
LN_EPS = 1e-5

kernel_name = "hybrid_hyena_diffattn_fnet_moe_dit"


def layer_norm(x, g=None, b=None):
    xf = x.astype(jnp.float32)
    mu = jnp.mean(xf, -1, keepdims=True)
    var = jnp.mean(jnp.square(xf - mu), -1, keepdims=True)
    y = (xf - mu) * lax.rsqrt(var + LN_EPS)
    if g is not None:
        y = y * g.astype(jnp.float32) + b.astype(jnp.float32)
    return y.astype(x.dtype)


def rms_norm(x, g):
    xf = x.astype(jnp.float32)
    y = xf * lax.rsqrt(jnp.mean(jnp.square(xf), -1, keepdims=True) + LN_EPS)
    return (y * g.astype(jnp.float32)).astype(x.dtype)


def short_conv(u, w):
    pad = SHORT_CONV // 2
    L = u.shape[1]
    up = jnp.pad(u, ((0, 0), (pad, pad), (0, 0)))
    out = up[:, 0:L] * w[0]
    for j in range(1, SHORT_CONV):
        out = out + up[:, j:j + L] * w[j]
    return out


def hyena_filters(L, w1, b1, w2, b2, w3, freq):
    f32 = jnp.float32
    w1, b1, w2, b2, w3, freq = (a.astype(f32) for a in (w1, b1, w2, b2, w3, freq))
    t = jnp.linspace(0.0, 1.0, L, dtype=f32)[:, None]
    bands = (HYENA_EMB - 1) // 2
    omega = (2.0 * math.pi / L) * jnp.arange(L, dtype=f32)[:, None]
    fb = jnp.linspace(1e-4, bands - 1, bands, dtype=f32)[None, :]
    z = jnp.concatenate([t, jnp.cos(fb * omega), -jnp.sin(fb * omega)], -1)
    a = jnp.sin(freq * (z @ w1 + b1))
    a = jnp.sin(freq * (a @ w2 + b2))
    hc = (a @ w3).reshape(L, HYENA_ORDER, 2, D_HYENA)
    deltas = jnp.abs(jnp.linspace(HYENA_MIN_DECAY, HYENA_MAX_DECAY, D_HYENA, dtype=f32))
    hc = hc * jnp.exp(-t * deltas)[:, None, None, :]
    l1 = jnp.abs(hc[:, :, 0]).sum(0) + jnp.abs(hc[1:, :, 1]).sum(0)
    return hc / l1[None, :, None, :]


def bidir_long_conv(u, h_fwd, h_bwd, bias):
    L = u.shape[1]
    k = jnp.concatenate([h_fwd, jnp.zeros_like(h_fwd[:1]), h_bwd[:0:-1]], 0)
    uf = u.astype(jnp.float32)
    y = jnp.fft.irfft(jnp.fft.rfft(uf, n=2 * L, axis=1) * jnp.fft.rfft(k, axis=0)[None],
                      n=2 * L, axis=1)[:, :L]
    return (y + uf * bias.astype(jnp.float32)).astype(u.dtype)


def hyena_mixer(u, p):
    L = u.shape[1]
    u = short_conv(u, p['short_conv_w'])
    x1, x2, v = jnp.split(u, 3, axis=-1)
    h = hyena_filters(L, p['filt_w1'], p['filt_b1'], p['filt_w2'], p['filt_b2'],
                      p['filt_w3'], p['filt_freq'])
    z = v
    for n, gate in enumerate((x1, x2)):
        z = gate * bidir_long_conv(z, h[:, n, 0], h[:, n, 1], p['hyena_bias'][n])
    return z


def axial_rope_angles(L):
    n = DIFF_QK // 4
    inv = 1.0 / (ROPE_BASE ** (jnp.arange(n, dtype=jnp.float32) / n))
    t = jnp.arange(L)
    row = (t // GRID_W).astype(jnp.float32)
    col = (t % GRID_W).astype(jnp.float32)
    ang = jnp.stack([row[:, None] * inv, col[:, None] * inv], axis=1)
    return jnp.cos(ang), jnp.sin(ang)


def apply_rope(t, cos, sin):
    shp = t.shape
    tr = t.reshape(*shp[:-1], 2, DIFF_QK // 4, 2).astype(jnp.float32)
    a, b = tr[..., 0], tr[..., 1]
    cs, sn = cos[:, None, None], sin[:, None, None]
    out = jnp.stack([a * cs - b * sn, a * sn + b * cs], -1)
    return out.reshape(shp).astype(t.dtype)


def qk_heads(t):
    B, L, _ = t.shape
    return t.reshape(B, L, N_DIFF_HEADS, 2, DIFF_QK)


def to_attn(t):
    return t.transpose(3, 0, 2, 1, 4)


def v_heads(t):
    B, L, _ = t.shape
    return t.reshape(B, L, N_DIFF_HEADS, DIFF_V).transpose(0, 2, 1, 3)


def diff_attn_core(q, k, v, lam):
    s = jnp.einsum('cbhqd,cbhkd->cbhqk', q, k).astype(jnp.float32) * (DIFF_QK ** -0.5)
    a = jax.nn.softmax(s, axis=-1)
    w = a[0] - lam * a[1]
    return jnp.einsum('bhqk,bhkd->bhqd', w.astype(v.dtype), v)


def blocked_diff_attention(q, k, v, lam):
    _, B, H, Lq, DQ = q.shape
    nb = Lq // ATTN_BLOCK
    qb = q.reshape(2, B, H, nb, ATTN_BLOCK, DQ).transpose(3, 0, 1, 2, 4, 5)
    out = lax.map(lambda qq: diff_attn_core(qq, k, v, lam), qb)
    return out.transpose(1, 2, 0, 3, 4).reshape(B, H, Lq, DIFF_V)


def diff_out(o, subln_w, lam_init):
    B, H, L, DV = o.shape
    o = rms_norm(o, subln_w) * (1.0 - lam_init)
    return o.transpose(0, 2, 1, 3).reshape(B, L, H * DV)


def fourier_mixer(u, w_lin):
    B, L, _ = u.shape
    ug = u.astype(jnp.float32).reshape(B, L, FNET_GROUPS, FNET_GROUP_W)
    f = jnp.fft.fftn(ug, axes=(1, 3), norm='ortho').real.astype(u.dtype)
    return jnp.einsum('blgc,gcd->blgd', f, w_lin).reshape(B, L, D_FNET)


def split_proj(z):
    return (z[..., :Q_OFF], z[..., Q_OFF:K_OFF], z[..., K_OFF:V_OFF],
            z[..., V_OFF:F_OFF], z[..., F_OFF:])


def token_mixer(h, hc, p, layer_idx, last):
    L = h.shape[1]
    lam_init = 0.8 - 0.6 * math.exp(-0.3 * layer_idx)
    f32 = jnp.float32
    lam = (jnp.exp(jnp.sum(p['lam_q1'].astype(f32) * p['lam_k1'].astype(f32)))
           - jnp.exp(jnp.sum(p['lam_q2'].astype(f32) * p['lam_k2'].astype(f32))) + lam_init)
    hy, q, k, v, fr = split_proj(h @ p['w_in'])
    cos, sin = axial_rope_angles(L)
    q = to_attn(apply_rope(qk_heads(q), cos, sin))
    k = to_attn(apply_rope(qk_heads(k), cos, sin))
    v = v_heads(v)
    if last:
        kvc = hc @ p['w_in'][:, K_OFF:F_OFF]
        kc, vc = kvc[..., :QK_W], kvc[..., QK_W:]
    else:
        hyc, qc, kc, vc, frc = split_proj(hc @ p['w_in'])
    kc = to_attn(qk_heads(kc))
    vc = v_heads(vc)
    att = blocked_diff_attention(q, jnp.concatenate([k, kc], axis=3),
                                 jnp.concatenate([v, vc], axis=2), lam)
    mixed = jnp.concatenate([hyena_mixer(hy, p), diff_out(att, p['subln_w'], lam_init),
                             fourier_mixer(fr, p['fnet_w'])], -1)
    out = mixed @ p['w_out']
    if last:
        return out, None
    att_c = diff_attn_core(to_attn(qk_heads(qc)), kc, vc, lam)
    mixed_c = jnp.concatenate([hyena_mixer(hyc, p), diff_out(att_c, p['subln_w'], lam_init),
                               fourier_mixer(frc, p['fnet_w'])], -1)
    return out, mixed_c @ p['w_out']


def moe_ffn(h, p):
    T, D = h.shape
    E, G = N_EXPERTS, MOE_BLOCK
    s = jax.nn.sigmoid((h @ p['router_w']).astype(jnp.float32))
    sel = s + p['router_b'].astype(jnp.float32)
    gscore = lax.top_k(sel.reshape(T, N_GROUPS, E // N_GROUPS), 2)[0].sum(-1)
    _, gidx = lax.top_k(gscore, TOPK_GROUPS)
    gmask = jnp.any(gidx[:, :, None] == jnp.arange(N_GROUPS)[None, None, :], axis=1)
    sel = jnp.where(jnp.repeat(gmask, E // N_GROUPS, axis=1), sel, -jnp.inf)
    _, eidx = lax.top_k(sel, TOP_K)
    ws = jnp.take_along_axis(s, eidx, axis=1)
    gates = ws / jnp.sum(ws, -1, keepdims=True) * ROUTED_SCALE
    A = T * TOP_K
    e_flat = eidx.reshape(-1)
    tok_flat = jnp.repeat(jnp.arange(T, dtype=jnp.int32), TOP_K)
    order = jnp.argsort(e_flat)
    e_s, tok_s, g_s = e_flat[order], tok_flat[order], gates.reshape(-1)[order]
    counts = jnp.bincount(e_flat, length=E)
    padded = (counts + G - 1) // G * G
    start = jnp.cumsum(counts) - counts
    cum_pad = jnp.cumsum(padded)
    pstart = cum_pad - padded
    pos = (pstart[e_s] + jnp.arange(A) - start[e_s]).astype(jnp.int32)
    n_blocks = (A + E * (G - 1) + G - 1) // G
    P = n_blocks * G
    rows_tok = jnp.full((P,), T, dtype=jnp.int32).at[pos].set(tok_s)
    gate_pad = jnp.zeros((P,), jnp.float32).at[pos].set(g_s)
    block_exp = jnp.minimum(jnp.searchsorted(cum_pad, jnp.arange(n_blocks) * G, side='right'),
                            E - 1)
    h_pad = jnp.concatenate([h, jnp.zeros((1, D), h.dtype)], 0)

    def expert_block(args):
        r, e = args
        xg = h_pad[r]
        a = jax.nn.silu(xg @ p['exp_w_gate'][e]) * (xg @ p['exp_w_up'][e])
        return a @ p['exp_w_down'][e]

    yb = lax.map(expert_block, (rows_tok.reshape(n_blocks, G), block_exp)).reshape(P, D)
    y = jnp.zeros((T + 1, D), jnp.float32).at[rows_tok].add(
        yb.astype(jnp.float32) * gate_pad[:, None])[:T]
    shared = (jax.nn.silu(h @ p['sh_w_gate']) * (h @ p['sh_w_up'])) @ p['sh_w_down']
    return y.astype(h.dtype) + shared


def trunk_layer(x, xc, c, c_ctx, p, layer_idx, last):
    D = D_MODEL
    B, L, _ = x.shape
    mod = (jax.nn.silu(c) @ p['w_mod'] + p['b_mod'])[:, None, :]
    sh1, sc1, g1, sh2, sc2, g2 = jnp.split(mod, 6, axis=-1)
    n_chunks = 2 if last else 6
    mod_c = jnp.split(jax.nn.silu(c_ctx) @ p['w_mod'][:, :n_chunks * D]
                      + p['b_mod'][:n_chunks * D], n_chunks)
    h = layer_norm(x) * (1.0 + sc1) + sh1
    hc = layer_norm(xc) * (1.0 + mod_c[1]) + mod_c[0]
    mix, mix_c = token_mixer(h, hc, p, layer_idx, last)
    x = layer_norm(ALPHA * x + g1 * mix, p['ln1_g'], p['ln1_b'])
    h = layer_norm(x) * (1.0 + sc2) + sh2
    if last:
        y = moe_ffn(h.reshape(B * L, D), p).reshape(B, L, D)
        return layer_norm(ALPHA * x + g2 * y, p['ln2_g'], p['ln2_b']), None
    cg1, csh2, csc2, cg2 = mod_c[2], mod_c[3], mod_c[4], mod_c[5]
    Lc = xc.shape[1]
    xc = layer_norm(ALPHA * xc + cg1 * mix_c, p['ln1_g'], p['ln1_b'])
    hc = layer_norm(xc) * (1.0 + csc2) + csh2
    y = moe_ffn(jnp.concatenate([h.reshape(B * L, D), hc.reshape(B * Lc, D)], 0), p)
    x = layer_norm(ALPHA * x + g2 * y[:B * L].reshape(B, L, D), p['ln2_g'], p['ln2_b'])
    xc = layer_norm(ALPHA * xc + cg2 * y[B * L:].reshape(B, Lc, D), p['ln2_g'], p['ln2_b'])
    return x, xc


def setup_inputs(seed: int = 0) -> dict:
    key = jax.random.key(seed)
    ks = iter(jax.random.split(key, 64))
    f32 = jnp.float32

    def nrm(shape, scale):
        return jax.random.normal(next(ks), shape, f32) * scale

    Dd, E = D_MODEL, N_EXPERTS
    col_scale = jnp.ones((D_IN,), f32).at[V_OFF:F_OFF].set(BETA)
    return {
        'x': nrm((BATCH, SEQ, Dd), 1.0),
        'c': nrm((BATCH, Dd), 1.0),
        'ctx': nrm((BATCH, CTX_LEN, Dd), 1.0),
        'c_ctx': nrm((Dd,), 1.0),
        'w_mod': nrm((DEPTH, Dd, 6 * Dd), 0.5 * Dd ** -0.5),
        'b_mod': nrm((DEPTH, 6 * Dd), 0.01),
        'w_in': nrm((DEPTH, Dd, D_IN), Dd ** -0.5) * col_scale,
        'w_out': nrm((DEPTH, D_MIX, Dd), BETA * D_MIX ** -0.5),
        'short_conv_w': nrm((DEPTH, SHORT_CONV, HY_W), SHORT_CONV ** -0.5),
        'filt_w1': nrm((DEPTH, HYENA_EMB, HYENA_FILTER_WIDTH), HYENA_EMB ** -0.5),
        'filt_b1': nrm((DEPTH, HYENA_FILTER_WIDTH), 0.01),
        'filt_w2': nrm((DEPTH, HYENA_FILTER_WIDTH, HYENA_FILTER_WIDTH), HYENA_FILTER_WIDTH ** -0.5),
        'filt_b2': nrm((DEPTH, HYENA_FILTER_WIDTH), 0.01),
        'filt_w3': nrm((DEPTH, HYENA_FILTER_WIDTH, HYENA_ORDER * 2 * D_HYENA), HYENA_FILTER_WIDTH ** -0.5),
        'filt_freq': 1.0 + nrm((DEPTH, HYENA_FILTER_WIDTH), 0.01),
        'hyena_bias': nrm((DEPTH, HYENA_ORDER, D_HYENA), 0.1),
        'lam_q1': nrm((DEPTH, DIFF_QK), 0.1),
        'lam_k1': nrm((DEPTH, DIFF_QK), 0.1),
        'lam_q2': nrm((DEPTH, DIFF_QK), 0.1),
        'lam_k2': nrm((DEPTH, DIFF_QK), 0.1),
        'subln_w': 1.0 + nrm((DEPTH, DIFF_V), 0.01),
        'fnet_w': nrm((DEPTH, FNET_GROUPS, FNET_GROUP_W, FNET_GROUP_W), FNET_GROUP_W ** -0.5),
        'ln1_g': 1.0 + nrm((DEPTH, Dd), 0.01),
        'ln1_b': nrm((DEPTH, Dd), 0.01),
        'ln2_g': 1.0 + nrm((DEPTH, Dd), 0.01),
        'ln2_b': nrm((DEPTH, Dd), 0.01),
        'router_w': nrm((DEPTH, Dd, E), Dd ** -0.5),
        'router_b': nrm((DEPTH, E), 0.01),
        'exp_w_gate': nrm((DEPTH, E, Dd, D_EXPERT), Dd ** -0.5),
        'exp_w_up': nrm((DEPTH, E, Dd, D_EXPERT), Dd ** -0.5),
        'exp_w_down': nrm((DEPTH, E, D_EXPERT, Dd), BETA * D_EXPERT ** -0.5),
        'sh_w_gate': nrm((DEPTH, Dd, D_SHARED), Dd ** -0.5),
        'sh_w_up': nrm((DEPTH, Dd, D_SHARED), Dd ** -0.5),
        'sh_w_down': nrm((DEPTH, D_SHARED, Dd), BETA * D_SHARED ** -0.5),
    }


def reference(x, c, ctx, c_ctx, w_mod, b_mod, w_in, w_out, short_conv_w, filt_w1, filt_b1,
              filt_w2, filt_b2, filt_w3, filt_freq, hyena_bias, lam_q1, lam_k1, lam_q2, lam_k2,
              subln_w, fnet_w, ln1_g, ln1_b, ln2_g, ln2_b, router_w, router_b, exp_w_gate,
              exp_w_up, exp_w_down, sh_w_gate, sh_w_up, sh_w_down):
    xc = ctx
    for i in range(DEPTH):
        p = {
            'w_mod': w_mod[i], 'b_mod': b_mod[i], 'w_in': w_in[i], 'w_out': w_out[i],
            'short_conv_w': short_conv_w[i], 'filt_w1': filt_w1[i], 'filt_b1': filt_b1[i],
            'filt_w2': filt_w2[i], 'filt_b2': filt_b2[i], 'filt_w3': filt_w3[i],
            'filt_freq': filt_freq[i], 'hyena_bias': hyena_bias[i],
            'lam_q1': lam_q1[i], 'lam_k1': lam_k1[i], 'lam_q2': lam_q2[i], 'lam_k2': lam_k2[i],
            'subln_w': subln_w[i], 'fnet_w': fnet_w[i],
            'ln1_g': ln1_g[i], 'ln1_b': ln1_b[i], 'ln2_g': ln2_g[i], 'ln2_b': ln2_b[i],
            'router_w': router_w[i], 'router_b': router_b[i],
            'exp_w_gate': exp_w_gate[i], 'exp_w_up': exp_w_up[i], 'exp_w_down': exp_w_down[i],
            'sh_w_gate': sh_w_gate[i], 'sh_w_up': sh_w_up[i], 'sh_w_down': sh_w_down[i],
        }
        x, xc = trunk_layer(x, xc, c, c_ctx, p, i, i == DEPTH - 1)
    return x
```

```python
import functools
import math

import jax
import jax.numpy as jnp
import numpy as np
from jax import lax
from jax.experimental import pallas as pl
from jax.experimental.pallas import tpu as pltpu

F32 = jnp.float32
BF16 = jnp.bfloat16
HIGHEST = lax.Precision.HIGHEST

D_MODEL = 1024
BATCH = 4
SEQ = 4096
DEPTH = 4
GRID_W = 64
CTX_LEN = 256
D_HYENA = 256
HYENA_ORDER = 2
HYENA_EMB = 33
HYENA_FILTER_WIDTH = 64
HYENA_TARGET = 1e-2
HYENA_MIN_DECAY = math.log(HYENA_TARGET) / 1.5
HYENA_MAX_DECAY = math.log(HYENA_TARGET) / 0.3
SHORT_CONV = 3
N_DIFF_HEADS = 8
DIFF_QK = 32
DIFF_V = 64
D_DIFF = N_DIFF_HEADS * DIFF_V
D_FNET = 256
FNET_GROUPS = 4
FNET_GROUP_W = D_FNET // FNET_GROUPS
D_MIX = D_HYENA + D_DIFF + D_FNET
ROPE_BASE = 10000.0
HY_W = 3 * D_HYENA
QK_W = N_DIFF_HEADS * 2 * DIFF_QK
Q_OFF = HY_W
K_OFF = Q_OFF + QK_W
V_OFF = K_OFF + QK_W
F_OFF = V_OFF + D_DIFF
D_IN = F_OFF + D_FNET
N_EXPERTS = 64
TOP_K = 8
N_GROUPS = 8
TOPK_GROUPS = 4
D_EXPERT = 256
D_SHARED = 256
ROUTED_SCALE = 2.5
ALPHA = (2 * DEPTH) ** 0.25
LN_EPS = 1e-5

T_LAT = BATCH * SEQ
T_CTX = BATCH * CTX_LEN
T_ALL = T_LAT + T_CTX

LANES = 128
VMEM_LIMIT = 56 * 1024 * 1024
ROW_TILE = 256
ATTN_TQ = 256
DFT_TILE = 512
ROUTE_TILE = 512
MOE_BLK = 256
COMBINE_TILE = 128
LOG2E = 1.4426950408889634


def _cparams(sem):
    return pltpu.CompilerParams(dimension_semantics=sem, vmem_limit_bytes=VMEM_LIMIT)


def _ln(x):
    mu = jnp.mean(x, -1, keepdims=True)
    xc = x - mu
    var = jnp.mean(xc * xc, -1, keepdims=True)
    return xc * lax.rsqrt(var + LN_EPS)


def _silu(x):
    return x * (1.0 / (1.0 + jnp.exp(-x)))


def _small_mm_kernel(a_ref, w_ref, b_ref, o_ref, *, silu_in):
    a = a_ref[...]
    if silu_in:
        a = _silu(a)
    o_ref[...] = jnp.dot(a, w_ref[...], precision=HIGHEST, preferred_element_type=F32) + b_ref[...]


def small_matmul(a, w, b=None, *, silu_in=False, tn=None):
    m, k = a.shape
    n = w.shape[1]
    tn = n if tn is None else tn
    if b is None:
        b = jnp.zeros((1, n), F32)
    return pl.pallas_call(
        functools.partial(_small_mm_kernel, silu_in=silu_in),
        grid=(n // tn,),
        in_specs=[pl.BlockSpec((m, k), lambda j: (0, 0)),
                  pl.BlockSpec((k, tn), lambda j: (0, j)),
                  pl.BlockSpec((1, tn), lambda j: (0, j))],
        out_specs=pl.BlockSpec((m, tn), lambda j: (0, j)),
        out_shape=jax.ShapeDtypeStruct((m, n), F32),
        compiler_params=_cparams(("arbitrary",)),
        name="small_matmul",
    )(a, w, b)


def _inproj_kernel(sid_ref, rid_ref, x_ref, mod_ref, w_ref, cos_ref, sin_ref,
                   hy_ref, q_ref, k_ref, v_ref, fr_ref):
    del sid_ref, rid_ref
    x = x_ref[...]
    sh = mod_ref[:, 0:D_MODEL]
    sc = mod_ref[:, D_MODEL:2 * D_MODEL]
    h = (_ln(x) * (1.0 + sc) + sh).astype(BF16)
    z = jnp.dot(h, w_ref[...], preferred_element_type=F32)
    hy_ref[...] = z[:, :HY_W]
    cos = jnp.concatenate([cos_ref[...]] * (QK_W // LANES), axis=1)
    sin = jnp.concatenate([sin_ref[...]] * (QK_W // LANES), axis=1)
    lane = lax.broadcasted_iota(jnp.int32, (1, QK_W), 1)
    even = (lane % 2) == 0

    def rope(t):
        nxt = pltpu.roll(t, QK_W - 1, 1)
        prv = pltpu.roll(t, 1, 1)
        return t * cos + jnp.where(even, nxt, prv) * sin

    qscale = (DIFF_QK ** -0.5) * LOG2E
    q_ref[...] = (rope(z[:, Q_OFF:K_OFF]) * qscale).astype(BF16)
    k_ref[...] = rope(z[:, K_OFF:V_OFF]).astype(BF16)
    v_ref[...] = z[:, V_OFF:F_OFF].astype(BF16)
    fr_ref[...] = z[:, F_OFF:].astype(BF16)


def in_projection(x, mod, w_in_bf16, cos_t, sin_t, sid, rid):
    t = x.shape[0]
    tm = ROW_TILE
    grid_spec = pltpu.PrefetchScalarGridSpec(
        num_scalar_prefetch=2,
        grid=(t // tm,),
        in_specs=[
            pl.BlockSpec((tm, D_MODEL), lambda i, sid, rid: (i, 0)),
            pl.BlockSpec((None, 1, 6 * D_MODEL), lambda i, sid, rid: (sid[i], 0, 0)),
            pl.BlockSpec((D_MODEL, D_IN), lambda i, sid, rid: (0, 0)),
            pl.BlockSpec((tm, LANES), lambda i, sid, rid: (rid[i], 0)),
            pl.BlockSpec((tm, LANES), lambda i, sid, rid: (rid[i], 0)),
        ],
        out_specs=[
            pl.BlockSpec((tm, HY_W), lambda i, sid, rid: (i, 0)),
            pl.BlockSpec((tm, QK_W), lambda i, sid, rid: (i, 0)),
            pl.BlockSpec((tm, QK_W), lambda i, sid, rid: (i, 0)),
            pl.BlockSpec((tm, D_DIFF), lambda i, sid, rid: (i, 0)),
            pl.BlockSpec((tm, D_FNET), lambda i, sid, rid: (i, 0)),
        ],
    )
    return pl.pallas_call(
        _inproj_kernel,
        grid_spec=grid_spec,
        out_shape=[jax.ShapeDtypeStruct((t, HY_W), F32),
                   jax.ShapeDtypeStruct((t, QK_W), BF16),
                   jax.ShapeDtypeStruct((t, QK_W), BF16),
                   jax.ShapeDtypeStruct((t, D_DIFF), BF16),
                   jax.ShapeDtypeStruct((t, D_FNET), BF16)],
        compiler_params=_cparams(("arbitrary",)),
        name="in_projection",
    )(sid, rid, x, mod, w_in_bf16, cos_t, sin_t)


def _attn_kernel(lam_ref, q_ref, *refs, lam_init, n_seg):
    k_refs = refs[:n_seg]
    v_refs = refs[n_seg:2 * n_seg]
    w_ref = refs[2 * n_seg]
    o_ref = refs[2 * n_seg + 1]
    q = q_ref[...]
    lane = lax.broadcasted_iota(jnp.int32, (1, LANES), 1)
    lam = lam_ref[0]
    nt = (((1,), (1,)), ((), ()))
    outs = []
    for h in range(2):
        probs = []
        for c in range(2):
            lo = h * DIFF_V + c * DIFF_QK
            qm = jnp.where((lane >= lo) & (lane < lo + DIFF_QK), q, jnp.zeros_like(q))
            s = [lax.dot_general(qm, kr[...], nt, preferred_element_type=F32) for kr in k_refs]
            mx = s[0].max(-1, keepdims=True)
            for sp in s[1:]:
                mx = jnp.maximum(mx, sp.max(-1, keepdims=True))
            e = [jnp.exp2(sp - mx) for sp in s]
            den = e[0].sum(-1, keepdims=True)
            for ep in e[1:]:
                den = den + ep.sum(-1, keepdims=True)
            probs.append((e, 1.0 / den))
        (e0, r0), (e1, r1) = probs
        r1 = lam * r1
        o = None
        for seg in range(n_seg):
            w = (e0[seg] * r0 - e1[seg] * r1).astype(BF16)
            part = jnp.dot(w, v_refs[seg][...], preferred_element_type=F32)
            o = part if o is None else o + part
        outs.append(o)
    first = lane < DIFF_V
    o = jnp.where(first, outs[0], outs[1])
    sq = o * o
    ms0 = jnp.sum(jnp.where(first, sq, 0.0), -1, keepdims=True)
    ms1 = jnp.sum(jnp.where(first, 0.0, sq), -1, keepdims=True)
    ms = jnp.where(first, ms0, ms1) * (1.0 / DIFF_V)
    y = o * lax.rsqrt(ms + LN_EPS) * w_ref[...] * (1.0 - lam_init)
    o_ref[...] = y.astype(o_ref.dtype)


def diff_attention(lam, q, k, v, subln2, *, lam_init, latent):
    tq = ATTN_TQ
    lat_blocks = SEQ // tq
    ctx_blocks = CTX_LEN // tq
    n_q = lat_blocks if latent else ctx_blocks
    q_base = 0 if latent else T_LAT // tq

    def q_map(b, hp, i):
        return (q_base + b * n_q + i, hp)

    def kl_map(b, hp, i):
        return (b, hp)

    def kc_map(b, hp, i):
        return (T_LAT // CTX_LEN + b, hp)

    seg_specs = []
    seg_args = []
    for arr in (k, v):
        if latent:
            seg_specs.append(pl.BlockSpec((SEQ, LANES), kl_map))
            seg_args.append(arr)
        seg_specs.append(pl.BlockSpec((CTX_LEN, LANES), kc_map))
        seg_args.append(arr)
    n_seg = 2 if latent else 1
    rows = T_LAT if latent else T_CTX
    out = pl.pallas_call(
        functools.partial(_attn_kernel, lam_init=lam_init, n_seg=n_seg),
        grid=(BATCH, N_DIFF_HEADS // 2, n_q),
        in_specs=[pl.BlockSpec(memory_space=pltpu.SMEM),
                  pl.BlockSpec((tq, LANES), q_map)] + seg_specs
                 + [pl.BlockSpec((1, LANES), lambda b, hp, i: (0, 0))],
        out_specs=pl.BlockSpec((tq, LANES), lambda b, hp, i: (b * n_q + i, hp)),
        out_shape=jax.ShapeDtypeStruct((rows, D_DIFF), BF16),
        compiler_params=_cparams(("arbitrary", "arbitrary", "arbitrary")),
        name="diff_attention_lat" if latent else "diff_attention_ctx",
    )(lam, q, *seg_args, subln2)
    return out


def _short_conv_kernel(u_ref, w_ref, o_ref):
    u = u_ref[...]
    n = u.shape[0]
    row = lax.broadcasted_iota(jnp.int32, (n, 1), 0)
    prv = jnp.where(row == 0, 0.0, pltpu.roll(u, 1, 0))
    nxt = jnp.where(row == n - 1, 0.0, pltpu.roll(u, n - 1, 0))
    o_ref[...] = prv * w_ref[0:1, :] + u * w_ref[1:2, :] + nxt * w_ref[2:3, :]


def short_conv(u, w):
    b, n, _ = u.shape
    return pl.pallas_call(
        _short_conv_kernel,
        grid=(b, 3),
        in_specs=[pl.BlockSpec((None, n, D_HYENA), lambda i, j: (i, 0, j)),
                  pl.BlockSpec((SHORT_CONV, D_HYENA), lambda i, j: (0, j))],
        out_specs=pl.BlockSpec((None, None, n, D_HYENA), lambda i, j: (j, i, 0, 0)),
        out_shape=jax.ShapeDtypeStruct((3, b, n, D_HYENA), F32),
        compiler_params=_cparams(("arbitrary", "arbitrary")),
        name="short_conv",
    )(u, w)


def _filter_kernel(z_ref, w1_ref, b1_ref, w2_ref, b2_ref, w3_ref, fq_ref, dl_ref, h_ref, ny_ref):
    n = z_ref.shape[0]
    fq = fq_ref[...]
    a = jnp.sin(fq * (jnp.dot(z_ref[...], w1_ref[...], precision=HIGHEST,
                              preferred_element_type=F32) + b1_ref[...]))
    a = jnp.sin(fq * (jnp.dot(a, w2_ref[...], precision=HIGHEST,
                              preferred_element_type=F32) + b2_ref[...]))
    row = lax.broadcasted_iota(jnp.int32, (n, 1), 0)
    t = row.astype(F32) * (1.0 / (n - 1))
    win = jnp.exp(-t * dl_ref[...])
    alt = jnp.where(row % 2 == 0, 1.0, -1.0)
    hf = jnp.dot(a, w3_ref[:, 0:D_HYENA], precision=HIGHEST, preferred_element_type=F32) * win
    hb = jnp.dot(a, w3_ref[:, D_HYENA:], precision=HIGHEST, preferred_element_type=F32) * win
    hb0 = jnp.where(row == 0, 0.0, hb)
    l1 = jnp.sum(jnp.abs(hf), 0, keepdims=True) + jnp.sum(jnp.abs(hb0), 0, keepdims=True)
    inv = 1.0 / l1
    hf = hf * inv
    hb0 = hb0 * inv
    ks = hf + hb0
    h_ref[:, 0:D_HYENA] = ks
    h_ref[:, D_HYENA:] = hf - hb0
    ny_ref[...] = jnp.sum(ks * alt, 0, keepdims=True)


def hyena_filters(n, zfeat, w1p, b1, w2, b2, w3, freq, deltas):
    full = lambda shape: pl.BlockSpec(shape, lambda o: (0,) * len(shape))
    return pl.pallas_call(
        _filter_kernel,
        grid=(HYENA_ORDER,),
        in_specs=[full(zfeat.shape), full(w1p.shape), full(b1.shape), full(w2.shape),
                  full(b2.shape),
                  pl.BlockSpec((HYENA_FILTER_WIDTH, 2 * D_HYENA), lambda o: (0, o)),
                  full(freq.shape), full(deltas.shape)],
        out_specs=[pl.BlockSpec((n, 2 * D_HYENA), lambda o: (0, o)),
                   pl.BlockSpec((None, 1, D_HYENA), lambda o: (o, 0, 0))],
        out_shape=[jax.ShapeDtypeStruct((n, 4 * D_HYENA), F32),
                   jax.ShapeDtypeStruct((HYENA_ORDER, 1, D_HYENA), F32)],
        compiler_params=_cparams(("arbitrary",)),
        name="hyena_filters",
    )(zfeat, w1p, b1, w2, b2, w3, freq, deltas)


def _dft_fwd_kernel(c_ref, s_ref, a_ref, b_ref, re_ref, im_ref):
    re_ref[...] = jnp.dot(c_ref[...], a_ref[...].astype(BF16), preferred_element_type=F32)
    im_ref[...] = jnp.dot(s_ref[...], b_ref[...].astype(BF16), preferred_element_type=F32)


def filter_spectrum(cmat, smat, h):
    n = cmat.shape[0]
    tf = min(DFT_TILE, n)
    return pl.pallas_call(
        _dft_fwd_kernel,
        grid=(n // tf, HYENA_ORDER),
        in_specs=[pl.BlockSpec((tf, n), lambda i, o: (i, 0)),
                  pl.BlockSpec((tf, n), lambda i, o: (i, 0)),
                  pl.BlockSpec((n, D_HYENA), lambda i, o: (0, 2 * o)),
                  pl.BlockSpec((n, D_HYENA), lambda i, o: (0, 2 * o + 1))],
        out_specs=[pl.BlockSpec((tf, D_HYENA), lambda i, o: (i, o)),
                   pl.BlockSpec((tf, D_HYENA), lambda i, o: (i, o))],
        out_shape=[jax.ShapeDtypeStruct((n, HYENA_ORDER * D_HYENA), F32)] * 2,
        compiler_params=_cparams(("arbitrary", "arbitrary")),
        name="filter_spectrum",
    )(cmat, smat, h, h)


def _spec_mul_kernel(c_ref, s_ref, z_ref, kre_ref, kim_ref, yre_ref, yim_ref, *, n):
    z = z_ref[...]
    ure = jnp.dot(c_ref[...], z, preferred_element_type=F32)
    uim = jnp.dot(s_ref[...], z, preferred_element_type=F32)
    kre = kre_ref[...]
    kim = kim_ref[...]
    tf = ure.shape[0]
    grow = pl.program_id(0) * tf + lax.broadcasted_iota(jnp.int32, (tf, 1), 0)
    dc = grow == 0
    yre = jnp.where(dc, ure * kre, ure * kre - uim * kim)
    yim = jnp.where(dc, uim * kim, ure * kim + uim * kre)
    wgt = jnp.where(dc, 1.0 / (2 * n), 2.0 / (2 * n))
    yre_ref[...] = (yre * wgt).astype(BF16)
    yim_ref[...] = (yim * wgt).astype(BF16)


def spectral_multiply(cmat, sfwd, z, kre, kim, order):
    b, n, _ = z.shape
    tf = min(DFT_TILE, n)
    return pl.pallas_call(
        functools.partial(_spec_mul_kernel, n=n),
        grid=(n // tf, b),
        in_specs=[pl.BlockSpec((tf, n), lambda i, j: (i, 0)),
                  pl.BlockSpec((tf, n), lambda i, j: (i, 0)),
                  pl.BlockSpec((None, n, D_HYENA), lambda i, j: (j, 0, 0)),
                  pl.BlockSpec((tf, D_HYENA), lambda i, j: (i, order)),
                  pl.BlockSpec((tf, D_HYENA), lambda i, j: (i, order))],
        out_specs=[pl.BlockSpec((None, tf, D_HYENA), lambda i, j: (j, i, 0)),
                   pl.BlockSpec((None, tf, D_HYENA), lambda i, j: (j, i, 0))],
        out_shape=[jax.ShapeDtypeStruct((b, n, D_HYENA), BF16)] * 2,
        compiler_params=_cparams(("arbitrary", "arbitrary")),
        name="spectral_multiply",
    )(cmat, sfwd, z, kre, kim)


def _inv_gate_kernel(c_ref, s_ref, yre_ref, yim_ref, z_ref, g_ref, bias_ref, o_ref):
    y = (jnp.dot(c_ref[...], yre_ref[...], preferred_element_type=F32)
         + jnp.dot(s_ref[...], yim_ref[...], preferred_element_type=F32))
    z = z_ref[...].astype(F32)
    o_ref[...] = (g_ref[...] * (y + z * bias_ref[...])).astype(o_ref.dtype)


def inverse_gate(cmat, sinv, yre, yim, z, gate, bias):
    b, n, _ = z.shape
    tt = min(DFT_TILE, n)
    return pl.pallas_call(
        _inv_gate_kernel,
        grid=(n // tt, b),
        in_specs=[pl.BlockSpec((tt, n), lambda i, j: (i, 0)),
                  pl.BlockSpec((tt, n), lambda i, j: (i, 0)),
                  pl.BlockSpec((None, n, D_HYENA), lambda i, j: (j, 0, 0)),
                  pl.BlockSpec((None, n, D_HYENA), lambda i, j: (j, 0, 0)),
                  pl.BlockSpec((None, tt, D_HYENA), lambda i, j: (j, i, 0)),
                  pl.BlockSpec((None, tt, D_HYENA), lambda i, j: (j, i, 0)),
                  pl.BlockSpec((1, D_HYENA), lambda i, j: (0, 0))],
        out_specs=pl.BlockSpec((None, tt, D_HYENA), lambda i, j: (j, i, 0)),
        out_shape=jax.ShapeDtypeStruct((b, n, D_HYENA), BF16),
        compiler_params=_cparams(("arbitrary", "arbitrary")),
        name="inverse_gate",
    )(cmat, sinv, yre, yim, z, gate, bias)


def _hyena_dft_mats(n):
    big = 2 * n
    f = jnp.arange(n, dtype=jnp.int32)
    m = (f[:, None] * f[None, :]) % big
    ang = m.astype(F32) * (2.0 * math.pi / big)
    c = jnp.cos(ang)
    s = -jnp.sin(ang)
    alt = jnp.where(f % 2 == 0, 1.0, -1.0).astype(F32)
    sfwd = jnp.where(f[:, None] == 0, alt[None, :], s)
    sinv = jnp.where(f[None, :] == 0, alt[:, None], s)
    return c.astype(BF16), sfwd.astype(BF16), sinv.astype(BF16)


def _fnet_dft_mats(n):
    f = jnp.arange(n, dtype=jnp.int32)
    m = (f[:, None] * f[None, :]) % n
    ang = m.astype(F32) * (2.0 * math.pi / n)
    return jnp.cos(ang).astype(BF16), (-jnp.sin(ang)).astype(BF16)


def _hyena_features(n):
    t = np.linspace(0.0, 1.0, n, dtype=np.float32)[:, None]
    bands = (HYENA_EMB - 1) // 2
    omega = (2.0 * math.pi / n) * np.arange(n, dtype=np.float32)[:, None]
    fb = np.linspace(1e-4, bands - 1, bands, dtype=np.float32)[None, :]
    z = np.concatenate([t, np.cos(fb * omega), -np.sin(fb * omega)], -1).astype(np.float32)
    return np.pad(z, ((0, 0), (0, LANES - HYENA_EMB)))


def hyena_mixer(u, p, mats):
    b, n, _ = u.shape
    cmat, sfwd, sinv = mats
    xs = short_conv(u, p['short_conv_w'])
    x1, x2, v = xs[0], xs[1], xs[2]
    deltas = np.abs(np.linspace(HYENA_MIN_DECAY, HYENA_MAX_DECAY, D_HYENA, dtype=np.float32))[None, :]
    w1p = jnp.pad(p['filt_w1'], ((0, LANES - HYENA_EMB), (0, 0)))
    h, nyq = hyena_filters(n, jnp.asarray(_hyena_features(n)), w1p, p['filt_b1'][None, :],
                           p['filt_w2'], p['filt_b2'][None, :], p['filt_w3'],
                           p['filt_freq'][None, :], jnp.asarray(deltas))
    kre, kim = filter_spectrum(cmat, sfwd, h)
    kim = kim.at[0].set(nyq.reshape(-1))
    z = v.astype(BF16)
    for o, gate in enumerate((x1, x2)):
        yre, yim = spectral_multiply(cmat, sfwd, z, kre, kim, o)
        z = inverse_gate(cmat, sinv, yre, yim, z, gate, p['hyena_bias'][o][None, :])
    return z


def _fnet_pq_kernel(x_ref, m_ref, p_ref, q_ref):
    pq = jnp.dot(x_ref[...], m_ref[...].astype(BF16), preferred_element_type=F32)
    p_ref[...] = pq[:, :D_FNET].astype(BF16)
    q_ref[...] = pq[:, D_FNET:].astype(BF16)


def _fnet_dft_kernel(c_ref, s_ref, p_ref, q_ref, o_ref):
    o_ref[...] = (jnp.dot(c_ref[...], p_ref[...], preferred_element_type=F32)
                  + jnp.dot(s_ref[...], q_ref[...], preferred_element_type=F32)).astype(o_ref.dtype)


def fourier_mixer(fr, fnet_w, mats):
    b, n, _ = fr.shape
    cmat, smat = mats
    cw = np.arange(FNET_GROUP_W)
    angw = 2.0 * math.pi * ((cw[:, None] * cw[None, :]) % FNET_GROUP_W) / FNET_GROUP_W
    scale = 1.0 / math.sqrt(n * FNET_GROUP_W)
    eye = np.eye(FNET_GROUPS)
    cbd = np.kron(eye, np.cos(angw) * scale).astype(np.float32)
    sbd = np.kron(eye, np.sin(angw) * scale).astype(np.float32)
    wbd = jnp.zeros((D_FNET, D_FNET), F32)
    for g in range(FNET_GROUPS):
        sl = slice(g * FNET_GROUP_W, (g + 1) * FNET_GROUP_W)
        wbd = wbd.at[sl, sl].set(fnet_w[g])
    mcs = small_matmul(jnp.asarray(np.concatenate([cbd, sbd], 0)), wbd)
    m = jnp.concatenate([mcs[:D_FNET], mcs[D_FNET:]], axis=1)
    tm = min(DFT_TILE, n)
    pm, qm = pl.pallas_call(
        _fnet_pq_kernel,
        grid=(b, n // tm),
        in_specs=[pl.BlockSpec((None, tm, D_FNET), lambda i, j: (i, j, 0)),
                  pl.BlockSpec((D_FNET, 2 * D_FNET), lambda i, j: (0, 0))],
        out_specs=[pl.BlockSpec((None, tm, D_FNET), lambda i, j: (i, j, 0))] * 2,
        out_shape=[jax.ShapeDtypeStruct((b, n, D_FNET), BF16)] * 2,
        compiler_params=_cparams(("arbitrary", "arbitrary")),
        name="fnet_channel_dft",
    )(fr, m)
    return pl.pallas_call(
        _fnet_dft_kernel,
        grid=(n // tm, b),
        in_specs=[pl.BlockSpec((tm, n), lambda i, j: (i, 0)),
                  pl.BlockSpec((tm, n), lambda i, j: (i, 0)),
                  pl.BlockSpec((None, n, D_FNET), lambda i, j: (j, 0, 0)),
                  pl.BlockSpec((None, n, D_FNET), lambda i, j: (j, 0, 0))],
        out_specs=pl.BlockSpec((None, tm, D_FNET), lambda i, j: (j, i, 0)),
        out_shape=jax.ShapeDtypeStruct((b, n, D_FNET), BF16),
        compiler_params=_cparams(("arbitrary", "arbitrary")),
        name="fnet_sequence_dft",
    )(cmat, smat, pm, qm)


def _outproj_kernel(sid_ref, hy_ref, at_ref, fn_ref, w_ref, x_ref, mod_ref, g_ref, b_ref, rw_ref,
                    x1_ref, h2_ref, lg_ref):
    del sid_ref
    mix = (jnp.dot(hy_ref[...], w_ref[0:D_HYENA, :], preferred_element_type=F32)
           + jnp.dot(at_ref[...], w_ref[D_HYENA:D_HYENA + D_DIFF, :], preferred_element_type=F32)
           + jnp.dot(fn_ref[...], w_ref[D_HYENA + D_DIFF:, :], preferred_element_type=F32))
    g1 = mod_ref[:, 2 * D_MODEL:3 * D_MODEL]
    sh2 = mod_ref[:, 3 * D_MODEL:4 * D_MODEL]
    sc2 = mod_ref[:, 4 * D_MODEL:5 * D_MODEL]
    x1 = _ln(ALPHA * x_ref[...] + g1 * mix) * g_ref[...] + b_ref[...]
    x1_ref[...] = x1
    h2 = _ln(x1) * (1.0 + sc2) + sh2
    h2_ref[...] = h2
    lg_ref[...] = lax.dot_general(rw_ref[...], h2, (((1,), (1,)), ((), ())),
                                  precision=HIGHEST, preferred_element_type=F32)


def out_projection(hy, at, fn, w_out_bf16, x, mod, ln_g, ln_b, router_wt, sid):
    t = x.shape[0]
    tm = ROW_TILE
    row = lambda w: pl.BlockSpec((tm, w), lambda i, sid: (i, 0))
    grid_spec = pltpu.PrefetchScalarGridSpec(
        num_scalar_prefetch=1,
        grid=(t // tm,),
        in_specs=[row(D_HYENA), row(D_DIFF), row(D_FNET),
                  pl.BlockSpec((D_MIX, D_MODEL), lambda i, sid: (0, 0)),
                  row(D_MODEL),
                  pl.BlockSpec((None, 1, 6 * D_MODEL), lambda i, sid: (sid[i], 0, 0)),
                  pl.BlockSpec((1, D_MODEL), lambda i, sid: (0, 0)),
                  pl.BlockSpec((1, D_MODEL), lambda i, sid: (0, 0)),
                  pl.BlockSpec((N_EXPERTS, D_MODEL), lambda i, sid: (0, 0))],
        out_specs=[row(D_MODEL), row(D_MODEL),
                   pl.BlockSpec((N_EXPERTS, tm), lambda i, sid: (0, i))],
    )
    return pl.pallas_call(
        _outproj_kernel,
        grid_spec=grid_spec,
        out_shape=[jax.ShapeDtypeStruct((t, D_MODEL), F32),
                   jax.ShapeDtypeStruct((t, D_MODEL), F32),
                   jax.ShapeDtypeStruct((N_EXPERTS, t), F32)],
        compiler_params=_cparams(("arbitrary",)),
        name="out_projection",
    )(sid, hy, at, fn, w_out_bf16, x, mod, ln_g, ln_b, router_wt)


def _route_kernel(lg_ref, bias_ref, tri_ref, eidx_ref, gate_ref, rank_ref, cnt_ref, carry_ref):
    @pl.when(pl.program_id(0) == 0)
    def _():
        carry_ref[...] = jnp.zeros_like(carry_ref)

    gsz = N_EXPERTS // N_GROUPS
    neg = -jnp.inf
    s = 1.0 / (1.0 + jnp.exp(-lg_ref[...]))
    sel = s + bias_ref[...]
    tt = s.shape[1]
    sub = lax.broadcasted_iota(jnp.int32, (gsz, tt), 0)
    gs = []
    for g in range(N_GROUPS):
        blk = sel[g * gsz:(g + 1) * gsz, :]
        m1 = blk.max(0, keepdims=True)
        first = jnp.min(jnp.where(blk == m1, sub, gsz), 0, keepdims=True)
        m2 = jnp.where(sub == first, neg, blk).max(0, keepdims=True)
        gs.append(m1 + m2)
    gscore = jnp.concatenate(gs, 0)
    gi = lax.broadcasted_iota(jnp.int32, (N_GROUPS, tt), 0)
    gkeep = jnp.zeros((N_GROUPS, tt), F32)
    cur = gscore
    for _ in range(TOPK_GROUPS):
        mx = cur.max(0, keepdims=True)
        idx = jnp.min(jnp.where(cur == mx, gi, N_GROUPS), 0, keepdims=True)
        pick = gi == idx
        gkeep = jnp.where(pick, 1.0, gkeep)
        cur = jnp.where(pick, neg, cur)
    keep = jnp.concatenate(
        [jnp.broadcast_to(gkeep[g:g + 1, :], (gsz, tt)) for g in range(N_GROUPS)], 0)
    cur = jnp.where(keep > 0.5, sel, neg)
    ei = lax.broadcasted_iota(jnp.int32, (N_EXPERTS, tt), 0)
    picks, idxs, ws = [], [], []
    onehot = jnp.zeros((N_EXPERTS, tt), F32)
    for _ in range(TOP_K):
        mx = cur.max(0, keepdims=True)
        idx = jnp.min(jnp.where(cur == mx, ei, N_EXPERTS), 0, keepdims=True)
        pick = ei == idx
        picks.append(pick)
        idxs.append(idx)
        ws.append(jnp.sum(jnp.where(pick, s, 0.0), 0, keepdims=True))
        cur = jnp.where(pick, neg, cur)
        onehot = jnp.where(pick, 1.0, onehot)
    wsum = ws[0]
    for w in ws[1:]:
        wsum = wsum + w
    inv = ROUTED_SCALE / wsum
    prefix = jnp.dot(onehot.astype(BF16), tri_ref[...], preferred_element_type=F32)
    rank_e = carry_ref[...] + prefix
    eidx_ref[...] = jnp.concatenate(idxs, 0)
    gate_ref[...] = jnp.concatenate([w * inv for w in ws], 0)
    rank_ref[...] = jnp.concatenate(
        [jnp.sum(jnp.where(pk, rank_e, 0.0), 0, keepdims=True) for pk in picks], 0).astype(jnp.int32)
    carry_ref[...] = carry_ref[...] + jnp.sum(onehot, 1, keepdims=True)
    cnt_ref[...] = carry_ref[...]


def route(logits_t, router_b):
    t = logits_t.shape[1]
    tt = ROUTE_TILE
    tri = jnp.asarray(np.triu(np.ones((tt, tt), np.float32), 1), BF16)
    col = lambda: pl.BlockSpec((TOP_K, tt), lambda i: (0, i))
    return pl.pallas_call(
        _route_kernel,
        grid=(t // tt,),
        in_specs=[pl.BlockSpec((N_EXPERTS, tt), lambda i: (0, i)),
                  pl.BlockSpec((N_EXPERTS, 1), lambda i: (0, 0)),
                  pl.BlockSpec((tt, tt), lambda i: (0, 0))],
        out_specs=[col(), col(), col(), pl.BlockSpec((N_EXPERTS, 1), lambda i: (0, 0))],
        out_shape=[jax.ShapeDtypeStruct((TOP_K, t), jnp.int32),
                   jax.ShapeDtypeStruct((TOP_K, t), F32),
                   jax.ShapeDtypeStruct((TOP_K, t), jnp.int32),
                   jax.ShapeDtypeStruct((N_EXPERTS, 1), F32)],
        scratch_shapes=[pltpu.VMEM((N_EXPERTS, 1), F32)],
        compiler_params=_cparams(("arbitrary",)),
        name="route",
    )(logits_t, router_b.reshape(N_EXPERTS, 1), tri)


def _row_copy(src, dst, sem):
    return pltpu.make_async_copy(src, dst, sem)


def _dispatch_kernel(pos_ref, h_ref, xg_in_ref, xg_ref, sem):
    del xg_in_ref
    tm = h_ref.shape[0]

    def issue(t, carry):
        for k in range(TOP_K):
            p = pos_ref[t * TOP_K + k]
            _row_copy(h_ref.at[pl.ds(t, 1), :], xg_ref.at[pl.ds(p, 1), :], sem).start()
        return carry

    lax.fori_loop(0, tm, issue, 0)

    def drain(t, carry):
        for k in range(TOP_K):
            _row_copy(h_ref.at[pl.ds(0, 1), :], xg_ref.at[pl.ds(0, 1), :], sem).wait()
        return carry

    lax.fori_loop(0, tm, drain, 0)


def dispatch(pos_flat, h2, xg):
    t = h2.shape[0]
    tm = ROW_TILE
    return pl.pallas_call(
        _dispatch_kernel,
        grid=(t // tm,),
        in_specs=[pl.BlockSpec((tm * TOP_K,), lambda i: (i,), memory_space=pltpu.SMEM),
                  pl.BlockSpec((tm, D_MODEL), lambda i: (i, 0)),
                  pl.BlockSpec(memory_space=pl.ANY)],
        out_specs=pl.BlockSpec(memory_space=pl.ANY),
        out_shape=jax.ShapeDtypeStruct(xg.shape, xg.dtype),
        scratch_shapes=[pltpu.SemaphoreType.DMA(())],
        input_output_aliases={2: 0},
        compiler_params=_cparams(("arbitrary",)),
        name="moe_dispatch",
    )(pos_flat, h2, xg)


def _expert_kernel(be_ref, nu_ref, x_ref, wg_ref, wu_ref, wd_ref, y_ref):
    del be_ref

    @pl.when(pl.program_id(0) < nu_ref[0])
    def _():
        x = x_ref[...].astype(BF16)
        a = (_silu(jnp.dot(x, wg_ref[...], preferred_element_type=F32))
             * jnp.dot(x, wu_ref[...], preferred_element_type=F32))
        y_ref[...] = jnp.dot(a.astype(BF16), wd_ref[...], preferred_element_type=F32)

    @pl.when(pl.program_id(0) >= nu_ref[0])
    def _():
        y_ref[...] = jnp.zeros_like(y_ref)


def expert_ffn(block_exp, n_used, xg, wg, wu, wd):
    p = xg.shape[0]
    blk = MOE_BLK
    grid_spec = pltpu.PrefetchScalarGridSpec(
        num_scalar_prefetch=2,
        grid=(p // blk,),
        in_specs=[pl.BlockSpec((blk, D_MODEL), lambda i, be, nu: (i, 0)),
                  pl.BlockSpec((None, D_MODEL, D_EXPERT), lambda i, be, nu: (be[i], 0, 0)),
                  pl.BlockSpec((None, D_MODEL, D_EXPERT), lambda i, be, nu: (be[i], 0, 0)),
                  pl.BlockSpec((None, D_EXPERT, D_MODEL), lambda i, be, nu: (be[i], 0, 0))],
        out_specs=pl.BlockSpec((blk, D_MODEL), lambda i, be, nu: (i, 0)),
    )
    return pl.pallas_call(
        _expert_kernel,
        grid_spec=grid_spec,
        out_shape=jax.ShapeDtypeStruct((p, D_MODEL), F32),
        compiler_params=_cparams(("arbitrary",)),
        name="expert_ffn",
    )(block_exp, n_used, xg, wg, wu, wd)


def _combine_kernel(sid_ref, pos_ref, yb_ref, gate_ref, h_ref, sg_ref, su_ref, sd_ref, x_ref,
                    mod_ref, g_ref, b_ref, o_ref, buf_ref, sem):
    del sid_ref
    tm = h_ref.shape[0]

    def issue(t, carry):
        for k in range(TOP_K):
            p = pos_ref[t * TOP_K + k]
            _row_copy(yb_ref.at[pl.ds(p, 1), :], buf_ref.at[k, pl.ds(t, 1), :], sem).start()
        return carry

    lax.fori_loop(0, tm, issue, 0)
    h = h_ref[...].astype(BF16)
    a = (_silu(jnp.dot(h, sg_ref[...], preferred_element_type=F32))
         * jnp.dot(h, su_ref[...], preferred_element_type=F32))
    y = jnp.dot(a.astype(BF16), sd_ref[...], preferred_element_type=F32)

    def drain(t, carry):
        for k in range(TOP_K):
            _row_copy(yb_ref.at[pl.ds(0, 1), :], buf_ref.at[0, pl.ds(0, 1), :], sem).wait()
        return carry

    lax.fori_loop(0, tm, drain, 0)
    gates = gate_ref[...]
    for k in range(TOP_K):
        y = y + buf_ref[k] * gates[:, k:k + 1]
    g2 = mod_ref[:, 5 * D_MODEL:6 * D_MODEL]
    o_ref[...] = _ln(ALPHA * x_ref[...] + g2 * y) * g_ref[...] + b_ref[...]


def combine(sid, pos_flat, yb, gates_tk, h2, sg, su, sd, x1, mod, ln_g, ln_b):
    t = h2.shape[0]
    tm = COMBINE_TILE
    row = lambda w: pl.BlockSpec((tm, w), lambda i, sid: (i, 0))
    const = lambda shape: pl.BlockSpec(shape, lambda i, sid: (0,) * len(shape))
    grid_spec = pltpu.PrefetchScalarGridSpec(
        num_scalar_prefetch=1,
        grid=(t // tm,),
        in_specs=[pl.BlockSpec((tm * TOP_K,), lambda i, sid: (i,), memory_space=pltpu.SMEM),
                  pl.BlockSpec(memory_space=pl.ANY),
                  row(TOP_K), row(D_MODEL),
                  const((D_MODEL, D_SHARED)), const((D_MODEL, D_SHARED)), const((D_SHARED, D_MODEL)),
                  row(D_MODEL),
                  pl.BlockSpec((None, 1, 6 * D_MODEL), lambda i, sid: (sid[i], 0, 0)),
                  const((1, D_MODEL)), const((1, D_MODEL))],
        out_specs=row(D_MODEL),
        scratch_shapes=[pltpu.VMEM((TOP_K, tm, D_MODEL), F32), pltpu.SemaphoreType.DMA(())],
    )
    return pl.pallas_call(
        _combine_kernel,
        grid_spec=grid_spec,
        out_shape=jax.ShapeDtypeStruct((t, D_MODEL), F32),
        compiler_params=_cparams(("arbitrary",)),
        name="moe_combine",
    )(sid, pos_flat, yb, gates_tk, h2, sg, su, sd, x1, mod, ln_g, ln_b)


def _moe_padded_rows(t):
    a = t * TOP_K
    n_blocks = (a + N_EXPERTS * (MOE_BLK - 1) + MOE_BLK - 1) // MOE_BLK
    return n_blocks, n_blocks * MOE_BLK


def moe_slots(eidx, rank, counts, t):
    n_blocks, _ = _moe_padded_rows(t)
    cnt = counts.reshape(-1).astype(jnp.int32)
    padded = (cnt + MOE_BLK - 1) // MOE_BLK * MOE_BLK
    cum_pad = jnp.cumsum(padded)
    pstart = cum_pad - padded
    pos = jnp.take(pstart, eidx) + rank
    block_exp = jnp.minimum(
        jnp.searchsorted(cum_pad, jnp.arange(n_blocks, dtype=jnp.int32) * MOE_BLK, side='right'),
        N_EXPERTS - 1).astype(jnp.int32)
    n_used = (cum_pad[-1] // MOE_BLK).astype(jnp.int32).reshape(1)
    return pos.T.reshape(-1), block_exp, n_used


def _rope_tables():
    n = DIFF_QK // 4
    inv = 1.0 / (ROPE_BASE ** (np.arange(n, dtype=np.float32) / n))
    t = np.arange(SEQ)
    row = (t // GRID_W).astype(np.float32)
    col = (t % GRID_W).astype(np.float32)
    ang = np.stack([row[:, None] * inv, col[:, None] * inv], axis=1).astype(np.float32)
    cos = np.repeat(np.cos(ang).reshape(SEQ, 2 * n), 2, axis=1)
    sin = np.repeat(np.sin(ang).reshape(SEQ, 2 * n), 2, axis=1)
    sign = np.tile(np.array([-1.0, 1.0], np.float32), DIFF_QK // 2)
    sin = sin * sign[None, :]
    cos = np.tile(cos, (1, LANES // DIFF_QK))
    sin = np.tile(sin, (1, LANES // DIFF_QK))
    cos = np.concatenate([cos, np.ones((ROW_TILE, LANES), np.float32)], 0)
    sin = np.concatenate([sin, np.zeros((ROW_TILE, LANES), np.float32)], 0)
    return jnp.asarray(cos, F32), jnp.asarray(sin, F32)


def _tile_ids(t, tile):
    n = t // tile
    i = np.arange(n)
    lat = T_LAT // tile
    sid = np.where(i < lat, i // (SEQ // tile), BATCH).astype(np.int32)
    rid = np.where(i < lat, i % (SEQ // tile), SEQ // tile).astype(np.int32)
    return jnp.asarray(sid), jnp.asarray(rid)


def kernel(x, c, ctx, c_ctx, w_mod, b_mod, w_in, w_out, short_conv_w, filt_w1, filt_b1, filt_w2,
           filt_b2, filt_w3, filt_freq, hyena_bias, lam_q1, lam_k1, lam_q2, lam_k2, subln_w, fnet_w,
           ln1_g, ln1_b, ln2_g, ln2_b, router_w, router_b, exp_w_gate, exp_w_up, exp_w_down,
           sh_w_gate, sh_w_up, sh_w_down):
    xt = jnp.concatenate([x.reshape(T_LAT, D_MODEL), ctx.reshape(T_CTX, D_MODEL)], 0)
    cc = jnp.concatenate([c, c_ctx[None, :], jnp.zeros((3, D_MODEL), F32)], 0)
    cos_t, sin_t = _rope_tables()
    sid_all, rid_all = _tile_ids(T_ALL, ROW_TILE)
    hy_mats = {SEQ: _hyena_dft_mats(SEQ), CTX_LEN: _hyena_dft_mats(CTX_LEN)}
    fn_mats = {SEQ: _fnet_dft_mats(SEQ), CTX_LEN: _fnet_dft_mats(CTX_LEN)}
    _, p_rows = _moe_padded_rows(T_ALL)
    xg = jnp.zeros((p_rows, D_MODEL), F32)

    for li in range(DEPTH):
        last = li == DEPTH - 1
        lam_init = 0.8 - 0.6 * math.exp(-0.3 * li)
        p = {'short_conv_w': short_conv_w[li], 'filt_w1': filt_w1[li], 'filt_b1': filt_b1[li],
             'filt_w2': filt_w2[li], 'filt_b2': filt_b2[li], 'filt_w3': filt_w3[li],
             'filt_freq': filt_freq[li], 'hyena_bias': hyena_bias[li]}
        mod = small_matmul(cc, w_mod[li], b_mod[li][None, :], silu_in=True, tn=6 * D_MODEL // 4)
        mod = mod.reshape(8, 1, 6 * D_MODEL)
        lam = (jnp.exp(jnp.sum(lam_q1[li] * lam_k1[li])) - jnp.exp(jnp.sum(lam_q2[li] * lam_k2[li]))
               + lam_init).reshape(1).astype(F32)

        hy, q, k, v, fr = in_projection(xt, mod, w_in[li].astype(BF16), cos_t, sin_t, sid_all, rid_all)
        subln2 = jnp.tile(subln_w[li], 2)[None, :]
        t_act = T_LAT if last else T_ALL
        att = diff_attention(lam, q, k, v, subln2, lam_init=lam_init, latent=True)
        hyo = hyena_mixer(hy[:T_LAT].reshape(BATCH, SEQ, HY_W), p, hy_mats[SEQ]).reshape(T_LAT, D_HYENA)
        fno = fourier_mixer(fr[:T_LAT].reshape(BATCH, SEQ, D_FNET), fnet_w[li],
                            fn_mats[SEQ]).reshape(T_LAT, D_FNET)
        if not last:
            att_c = diff_attention(lam, q, k, v, subln2, lam_init=lam_init, latent=False)
            hyo_c = hyena_mixer(hy[T_LAT:].reshape(BATCH, CTX_LEN, HY_W), p, hy_mats[CTX_LEN])
            fno_c = fourier_mixer(fr[T_LAT:].reshape(BATCH, CTX_LEN, D_FNET), fnet_w[li],
                                  fn_mats[CTX_LEN])
            att = jnp.concatenate([att, att_c], 0)
            hyo = jnp.concatenate([hyo, hyo_c.reshape(T_CTX, D_HYENA)], 0)
            fno = jnp.concatenate([fno, fno_c.reshape(T_CTX, D_FNET)], 0)

        sid_r, _ = _tile_ids(t_act, ROW_TILE)
        x1, h2, logits_t = out_projection(
            hyo, att, fno, w_out[li].astype(BF16), xt[:t_act], mod,
            ln1_g[li][None, :], ln1_b[li][None, :], router_w[li].T, sid_r)

        eidx, gates, rank, counts = route(logits_t, router_b[li])
        pos_flat, block_exp, n_used = moe_slots(eidx, rank, counts, T_ALL)
        xg = dispatch(pos_flat, h2, xg)
        yb = expert_ffn(block_exp, n_used, xg, exp_w_gate[li].astype(BF16),
                        exp_w_up[li].astype(BF16), exp_w_down[li].astype(BF16))
        sid_c, _ = _tile_ids(t_act, COMBINE_TILE)
        xt = combine(sid_c, pos_flat, yb, gates.T, h2, sh_w_gate[li].astype(BF16),
                     sh_w_up[li].astype(BF16), sh_w_down[li].astype(BF16), x1, mod,
                     ln2_g[li][None, :], ln2_b[li][None, :])
    return xt[:T_LAT].reshape(BATCH, SEQ, D_MODEL)
```

```python
import functools
import math

import jax
import jax.numpy as jnp
import numpy as np
from jax import lax
from jax.experimental import pallas as pl
from jax.experimental.pallas import tpu as pltpu

F32 = jnp.float32
BF16 = jnp.bfloat16
HIGHEST = lax.Precision.HIGHEST

D_MODEL = 1024
BATCH = 4
SEQ = 4096
DEPTH = 4
GRID_W = 64
CTX_LEN = 256
D_HYENA = 256
HYENA_ORDER = 2
HYENA_EMB = 33
HYENA_FILTER_WIDTH = 64
HYENA_TARGET = 1e-2
HYENA_MIN_DECAY = math.log(HYENA_TARGET) / 1.5
HYENA_MAX_DECAY = math.log(HYENA_TARGET) / 0.3
SHORT_CONV = 3
N_DIFF_HEADS = 8
DIFF_QK = 32
DIFF_V = 64
D_DIFF = N_DIFF_HEADS * DIFF_V
D_FNET = 256
FNET_GROUPS = 4
FNET_GROUP_W = D_FNET // FNET_GROUPS
D_MIX = D_HYENA + D_DIFF + D_FNET
ROPE_BASE = 10000.0
HY_W = 3 * D_HYENA
QK_W = N_DIFF_HEADS * 2 * DIFF_QK
Q_OFF = HY_W
K_OFF = Q_OFF + QK_W
V_OFF = K_OFF + QK_W
F_OFF = V_OFF + D_DIFF
D_IN = F_OFF + D_FNET
N_EXPERTS = 64
TOP_K = 8
N_GROUPS = 8
TOPK_GROUPS = 4
D_EXPERT = 256
D_SHARED = 256
ROUTED_SCALE = 2.5
ALPHA = (2 * DEPTH) ** 0.25
LN_EPS = 1e-5

T_LAT = BATCH * SEQ
T_CTX = BATCH * CTX_LEN
T_ALL = T_LAT + T_CTX

LANES = 128
VMEM_LIMIT = 56 * 1024 * 1024
ROW_TILE = 256
ATTN_TQ = 256
DFT_TILE = 512
ROUTE_TILE = 512
MOE_BLK = 256
COMBINE_TILE = 128
LOG2E = 1.4426950408889634


def _cparams(sem):
    return pltpu.CompilerParams(dimension_semantics=sem, vmem_limit_bytes=VMEM_LIMIT)


def _ln(x):
    mu = jnp.mean(x, -1, keepdims=True)
    xc = x - mu
    var = jnp.mean(xc * xc, -1, keepdims=True)
    return xc * lax.rsqrt(var + LN_EPS)


def _silu(x):
    return x * (1.0 / (1.0 + jnp.exp(-x)))


def _small_mm_kernel(a_ref, w_ref, b_ref, o_ref, *, silu_in):
    a = a_ref[...]
    if silu_in:
        a = _silu(a)
    o_ref[...] = jnp.dot(a, w_ref[...], precision=HIGHEST, preferred_element_type=F32) + b_ref[...]


def small_matmul(a, w, b=None, *, silu_in=False, tn=None):
    m, k = a.shape
    n = w.shape[1]
    tn = n if tn is None else tn
    if b is None:
        b = jnp.zeros((1, n), F32)
    return pl.pallas_call(
        functools.partial(_small_mm_kernel, silu_in=silu_in),
        grid=(n // tn,),
        in_specs=[pl.BlockSpec((m, k), lambda j: (0, 0)),
                  pl.BlockSpec((k, tn), lambda j: (0, j)),
                  pl.BlockSpec((1, tn), lambda j: (0, j))],
        out_specs=pl.BlockSpec((m, tn), lambda j: (0, j)),
        out_shape=jax.ShapeDtypeStruct((m, n), F32),
        compiler_params=_cparams(("arbitrary",)),
        name="small_matmul",
    )(a, w, b)


def _inproj_kernel(sid_ref, rid_ref, x_ref, mod_ref, w_ref, cos_ref, sin_ref,
                   hy_ref, q_ref, k_ref, v_ref, fr_ref):
    del sid_ref, rid_ref
    x = x_ref[...]
    sh = mod_ref[:, 0:D_MODEL]
    sc = mod_ref[:, D_MODEL:2 * D_MODEL]
    h = (_ln(x) * (1.0 + sc) + sh).astype(BF16)
    z = jnp.dot(h, w_ref[...], preferred_element_type=F32)
    hy_ref[...] = z[:, :HY_W]
    cos = jnp.concatenate([cos_ref[...]] * (QK_W // LANES), axis=1)
    sin = jnp.concatenate([sin_ref[...]] * (QK_W // LANES), axis=1)
    lane = lax.broadcasted_iota(jnp.int32, (1, QK_W), 1)
    even = (lane % 2) == 0

    def rope(t):
        nxt = pltpu.roll(t, QK_W - 1, 1)
        prv = pltpu.roll(t, 1, 1)
        return t * cos + jnp.where(even, nxt, prv) * sin

    qscale = (DIFF_QK ** -0.5) * LOG2E
    q_ref[...] = (rope(z[:, Q_OFF:K_OFF]) * qscale).astype(BF16)
    k_ref[...] = rope(z[:, K_OFF:V_OFF]).astype(BF16)
    v_ref[...] = z[:, V_OFF:F_OFF].T.astype(BF16)
    fr_ref[...] = z[:, F_OFF:].astype(BF16)


def in_projection(x, mod, w_in_bf16, cos_t, sin_t, sid, rid):
    t = x.shape[0]
    tm = ROW_TILE
    grid_spec = pltpu.PrefetchScalarGridSpec(
        num_scalar_prefetch=2,
        grid=(t // tm,),
        in_specs=[
            pl.BlockSpec((tm, D_MODEL), lambda i, sid, rid: (i, 0)),
            pl.BlockSpec((None, 1, 6 * D_MODEL), lambda i, sid, rid: (sid[i], 0, 0)),
            pl.BlockSpec((D_MODEL, D_IN), lambda i, sid, rid: (0, 0)),
            pl.BlockSpec((tm, LANES), lambda i, sid, rid: (rid[i], 0)),
            pl.BlockSpec((tm, LANES), lambda i, sid, rid: (rid[i], 0)),
        ],
        out_specs=[
            pl.BlockSpec((tm, HY_W), lambda i, sid, rid: (i, 0)),
            pl.BlockSpec((tm, QK_W), lambda i, sid, rid: (i, 0)),
            pl.BlockSpec((tm, QK_W), lambda i, sid, rid: (i, 0)),
            pl.BlockSpec((D_DIFF, tm), lambda i, sid, rid: (0, i)),
            pl.BlockSpec((tm, D_FNET), lambda i, sid, rid: (i, 0)),
        ],
    )
    return pl.pallas_call(
        _inproj_kernel,
        grid_spec=grid_spec,
        out_shape=[jax.ShapeDtypeStruct((t, HY_W), F32),
                   jax.ShapeDtypeStruct((t, QK_W), BF16),
                   jax.ShapeDtypeStruct((t, QK_W), BF16),
                   jax.ShapeDtypeStruct((D_DIFF, t), BF16),
                   jax.ShapeDtypeStruct((t, D_FNET), BF16)],
        compiler_params=_cparams(("arbitrary",)),
        name="in_projection",
    )(sid, rid, x, mod, w_in_bf16, cos_t, sin_t)


def _attn_kernel(lam_ref, q_ref, *refs, lam_init, n_seg):
    k_refs = refs[:n_seg]
    vt_refs = refs[n_seg:2 * n_seg]
    w_ref = refs[2 * n_seg]
    o_ref = refs[2 * n_seg + 1]
    q = q_ref[...]
    lane = lax.broadcasted_iota(jnp.int32, (1, LANES), 1)
    row = lax.broadcasted_iota(jnp.int32, (LANES, 1), 0)
    lam = lam_ref[0]
    nt = (((1,), (1,)), ((), ()))
    heads = []
    for h in range(2):
        in_head = (row >= h * DIFF_V) & (row < (h + 1) * DIFF_V)
        vts = [jnp.where(in_head, vr[...], jnp.ones((), BF16)) for vr in vt_refs]
        acc = []
        for c in range(2):
            lo = h * DIFF_V + c * DIFF_QK
            qm = jnp.where((lane >= lo) & (lane < lo + DIFF_QK), q, jnp.zeros_like(q))
            s = [lax.dot_general(kr[...], qm, nt, preferred_element_type=F32) for kr in k_refs]
            mx = s[0].max(0, keepdims=True)
            for sp in s[1:]:
                mx = jnp.maximum(mx, sp.max(0, keepdims=True))
            o = None
            for seg in range(n_seg):
                e = jnp.exp2(s[seg] - mx).astype(BF16)
                part = jnp.dot(vts[seg], e, preferred_element_type=F32)
                o = part if o is None else o + part
            acc.append(o)
        drow = (1 - h) * DIFF_V
        r0 = 1.0 / acc[0][drow:drow + 1, :]
        r1 = lam / acc[1][drow:drow + 1, :]
        heads.append(acc[0] * r0 - acc[1] * r1)
    o = jnp.where(row < DIFF_V, heads[0], heads[1]).T
    first = lane < DIFF_V
    sq = o * o
    ms0 = jnp.sum(jnp.where(first, sq, 0.0), -1, keepdims=True)
    ms1 = jnp.sum(jnp.where(first, 0.0, sq), -1, keepdims=True)
    ms = jnp.where(first, ms0, ms1) * (1.0 / DIFF_V)
    y = o * lax.rsqrt(ms + LN_EPS) * w_ref[...] * (1.0 - lam_init)
    o_ref[...] = y.astype(o_ref.dtype)


def diff_attention(lam, q, k, vt, subln2, *, lam_init, latent):
    tq = ATTN_TQ
    n_q = (SEQ if latent else CTX_LEN) // tq
    q_base = 0 if latent else T_LAT // tq
    ctx_blk = T_LAT // CTX_LEN

    specs, args = [], []
    if latent:
        specs.append(pl.BlockSpec((SEQ, LANES), lambda b, hp, i: (b, hp)))
        args.append(k)
    specs.append(pl.BlockSpec((CTX_LEN, LANES), lambda b, hp, i: (ctx_blk + b, hp)))
    args.append(k)
    if latent:
        specs.append(pl.BlockSpec((LANES, SEQ), lambda b, hp, i: (hp, b)))
        args.append(vt)
    specs.append(pl.BlockSpec((LANES, CTX_LEN), lambda b, hp, i: (hp, ctx_blk + b)))
    args.append(vt)
    n_seg = 2 if latent else 1
    rows = T_LAT if latent else T_CTX
    return pl.pallas_call(
        functools.partial(_attn_kernel, lam_init=lam_init, n_seg=n_seg),
        grid=(BATCH, N_DIFF_HEADS // 2, n_q),
        in_specs=[pl.BlockSpec(memory_space=pltpu.SMEM),
                  pl.BlockSpec((tq, LANES), lambda b, hp, i: (q_base + b * n_q + i, hp))] + specs
                 + [pl.BlockSpec((1, LANES), lambda b, hp, i: (0, 0))],
        out_specs=pl.BlockSpec((tq, LANES), lambda b, hp, i: (b * n_q + i, hp)),
        out_shape=jax.ShapeDtypeStruct((rows, D_DIFF), BF16),
        compiler_params=_cparams(("arbitrary", "arbitrary", "arbitrary")),
        name="diff_attention_lat" if latent else "diff_attention_ctx",
    )(lam, q, *args, subln2)


def _short_conv_kernel(u_ref, w_ref, o_ref):
    u = u_ref[...]
    n = u.shape[0]
    row = lax.broadcasted_iota(jnp.int32, (n, 1), 0)
    prv = jnp.where(row == 0, 0.0, pltpu.roll(u, 1, 0))
    nxt = jnp.where(row == n - 1, 0.0, pltpu.roll(u, n - 1, 0))
    o_ref[...] = prv * w_ref[0:1, :] + u * w_ref[1:2, :] + nxt * w_ref[2:3, :]


def short_conv(u, w):
    b, n, _ = u.shape
    return pl.pallas_call(
        _short_conv_kernel,
        grid=(b, 3),
        in_specs=[pl.BlockSpec((None, n, D_HYENA), lambda i, j: (i, 0, j)),
                  pl.BlockSpec((SHORT_CONV, D_HYENA), lambda i, j: (0, j))],
        out_specs=pl.BlockSpec((None, None, n, D_HYENA), lambda i, j: (j, i, 0, 0)),
        out_shape=jax.ShapeDtypeStruct((3, b, n, D_HYENA), F32),
        compiler_params=_cparams(("arbitrary", "arbitrary")),
        name="short_conv",
    )(u, w)


def _filter_kernel(z_ref, w1_ref, b1_ref, w2_ref, b2_ref, w3_ref, fq_ref, dl_ref, h_ref, ny_ref):
    n = z_ref.shape[0]
    fq = fq_ref[...]
    a = jnp.sin(fq * (jnp.dot(z_ref[...], w1_ref[...], precision=HIGHEST,
                              preferred_element_type=F32) + b1_ref[...]))
    a = jnp.sin(fq * (jnp.dot(a, w2_ref[...], precision=HIGHEST,
                              preferred_element_type=F32) + b2_ref[...]))
    row = lax.broadcasted_iota(jnp.int32, (n, 1), 0)
    t = row.astype(F32) * (1.0 / (n - 1))
    win = jnp.exp(-t * dl_ref[...])
    alt = jnp.where(row % 2 == 0, 1.0, -1.0)
    hf = jnp.dot(a, w3_ref[:, 0:D_HYENA], precision=HIGHEST, preferred_element_type=F32) * win
    hb = jnp.dot(a, w3_ref[:, D_HYENA:], precision=HIGHEST, preferred_element_type=F32) * win
    hb0 = jnp.where(row == 0, 0.0, hb)
    l1 = jnp.sum(jnp.abs(hf), 0, keepdims=True) + jnp.sum(jnp.abs(hb0), 0, keepdims=True)
    inv = 1.0 / l1
    hf = hf * inv
    hb0 = hb0 * inv
    ks = hf + hb0
    h_ref[:, 0:D_HYENA] = ks
    h_ref[:, D_HYENA:] = hf - hb0
    ny_ref[...] = jnp.sum(ks * alt, 0, keepdims=True)


def hyena_filters(n, zfeat, w1p, b1, w2, b2, w3, freq, deltas):
    full = lambda shape: pl.BlockSpec(shape, lambda o: (0,) * len(shape))
    return pl.pallas_call(
        _filter_kernel,
        grid=(HYENA_ORDER,),
        in_specs=[full(zfeat.shape), full(w1p.shape), full(b1.shape), full(w2.shape),
                  full(b2.shape),
                  pl.BlockSpec((HYENA_FILTER_WIDTH, 2 * D_HYENA), lambda o: (0, o)),
                  full(freq.shape), full(deltas.shape)],
        out_specs=[pl.BlockSpec((n, 2 * D_HYENA), lambda o: (0, o)),
                   pl.BlockSpec((None, 1, D_HYENA), lambda o: (o, 0, 0))],
        out_shape=[jax.ShapeDtypeStruct((n, 4 * D_HYENA), F32),
                   jax.ShapeDtypeStruct((HYENA_ORDER, 1, D_HYENA), F32)],
        compiler_params=_cparams(("arbitrary",)),
        name="hyena_filters",
    )(zfeat, w1p, b1, w2, b2, w3, freq, deltas)


def _dft_fwd_kernel(c_ref, s_ref, a_ref, b_ref, re_ref, im_ref):
    re_ref[...] = jnp.dot(c_ref[...], a_ref[...].astype(BF16), preferred_element_type=F32)
    im_ref[...] = jnp.dot(s_ref[...], b_ref[...].astype(BF16), preferred_element_type=F32)


def filter_spectrum(cmat, smat, h):
    n = cmat.shape[0]
    tf = min(DFT_TILE, n)
    return pl.pallas_call(
        _dft_fwd_kernel,
        grid=(n // tf, HYENA_ORDER),
        in_specs=[pl.BlockSpec((tf, n), lambda i, o: (i, 0)),
                  pl.BlockSpec((tf, n), lambda i, o: (i, 0)),
                  pl.BlockSpec((n, D_HYENA), lambda i, o: (0, 2 * o)),
                  pl.BlockSpec((n, D_HYENA), lambda i, o: (0, 2 * o + 1))],
        out_specs=[pl.BlockSpec((tf, D_HYENA), lambda i, o: (i, o)),
                   pl.BlockSpec((tf, D_HYENA), lambda i, o: (i, o))],
        out_shape=[jax.ShapeDtypeStruct((n, HYENA_ORDER * D_HYENA), F32)] * 2,
        compiler_params=_cparams(("arbitrary", "arbitrary")),
        name="filter_spectrum",
    )(cmat, smat, h, h)


def _spec_mul_kernel(c_ref, s_ref, z_ref, kre_ref, kim_ref, yre_ref, yim_ref, *, n):
    z = z_ref[...]
    ure = jnp.dot(c_ref[...], z, preferred_element_type=F32)
    uim = jnp.dot(s_ref[...], z, preferred_element_type=F32)
    kre = kre_ref[...]
    kim = kim_ref[...]
    tf = ure.shape[0]
    grow = pl.program_id(0) * tf + lax.broadcasted_iota(jnp.int32, (tf, 1), 0)
    dc = grow == 0
    yre = jnp.where(dc, ure * kre, ure * kre - uim * kim)
    yim = jnp.where(dc, uim * kim, ure * kim + uim * kre)
    wgt = jnp.where(dc, 1.0 / (2 * n), 2.0 / (2 * n))
    yre_ref[...] = (yre * wgt).astype(BF16)
    yim_ref[...] = (yim * wgt).astype(BF16)


def spectral_multiply(cmat, sfwd, z, kre, kim, order):
    b, n, _ = z.shape
    tf = min(DFT_TILE, n)
    return pl.pallas_call(
        functools.partial(_spec_mul_kernel, n=n),
        grid=(n // tf, b),
        in_specs=[pl.BlockSpec((tf, n), lambda i, j: (i, 0)),
                  pl.BlockSpec((tf, n), lambda i, j: (i, 0)),
                  pl.BlockSpec((None, n, D_HYENA), lambda i, j: (j, 0, 0)),
                  pl.BlockSpec((tf, D_HYENA), lambda i, j: (i, order)),
                  pl.BlockSpec((tf, D_HYENA), lambda i, j: (i, order))],
        out_specs=[pl.BlockSpec((None, tf, D_HYENA), lambda i, j: (j, i, 0)),
                   pl.BlockSpec((None, tf, D_HYENA), lambda i, j: (j, i, 0))],
        out_shape=[jax.ShapeDtypeStruct((b, n, D_HYENA), BF16)] * 2,
        compiler_params=_cparams(("arbitrary", "arbitrary")),
        name="spectral_multiply",
    )(cmat, sfwd, z, kre, kim)


def _inv_gate_kernel(c_ref, s_ref, yre_ref, yim_ref, z_ref, g_ref, bias_ref, o_ref):
    y = (jnp.dot(c_ref[...], yre_ref[...], preferred_element_type=F32)
         + jnp.dot(s_ref[...], yim_ref[...], preferred_element_type=F32))
    z = z_ref[...].astype(F32)
    o_ref[...] = (g_ref[...] * (y + z * bias_ref[...])).astype(o_ref.dtype)


def inverse_gate(cmat, sinv, yre, yim, z, gate, bias):
    b, n, _ = z.shape
    tt = min(DFT_TILE, n)
    return pl.pallas_call(
        _inv_gate_kernel,
        grid=(n // tt, b),
        in_specs=[pl.BlockSpec((tt, n), lambda i, j: (i, 0)),
                  pl.BlockSpec((tt, n), lambda i, j: (i, 0)),
                  pl.BlockSpec((None, n, D_HYENA), lambda i, j: (j, 0, 0)),
                  pl.BlockSpec((None, n, D_HYENA), lambda i, j: (j, 0, 0)),
                  pl.BlockSpec((None, tt, D_HYENA), lambda i, j: (j, i, 0)),
                  pl.BlockSpec((None, tt, D_HYENA), lambda i, j: (j, i, 0)),
                  pl.BlockSpec((1, D_HYENA), lambda i, j: (0, 0))],
        out_specs=pl.BlockSpec((None, tt, D_HYENA), lambda i, j: (j, i, 0)),
        out_shape=jax.ShapeDtypeStruct((b, n, D_HYENA), BF16),
        compiler_params=_cparams(("arbitrary", "arbitrary")),
        name="inverse_gate",
    )(cmat, sinv, yre, yim, z, gate, bias)


def _hyena_dft_mats(n):
    big = 2 * n
    f = jnp.arange(n, dtype=jnp.int32)
    m = (f[:, None] * f[None, :]) % big
    ang = m.astype(F32) * (2.0 * math.pi / big)
    c = jnp.cos(ang)
    s = -jnp.sin(ang)
    alt = jnp.where(f % 2 == 0, 1.0, -1.0).astype(F32)
    sfwd = jnp.where(f[:, None] == 0, alt[None, :], s)
    sinv = jnp.where(f[None, :] == 0, alt[:, None], s)
    return c.astype(BF16), sfwd.astype(BF16), sinv.astype(BF16)


def _fnet_dft_mats(n):
    f = jnp.arange(n, dtype=jnp.int32)
    m = (f[:, None] * f[None, :]) % n
    ang = m.astype(F32) * (2.0 * math.pi / n)
    return jnp.cos(ang).astype(BF16), (-jnp.sin(ang)).astype(BF16)


def _hyena_features(n):
    t = np.linspace(0.0, 1.0, n, dtype=np.float32)[:, None]
    bands = (HYENA_EMB - 1) // 2
    omega = (2.0 * math.pi / n) * np.arange(n, dtype=np.float32)[:, None]
    fb = np.linspace(1e-4, bands - 1, bands, dtype=np.float32)[None, :]
    z = np.concatenate([t, np.cos(fb * omega), -np.sin(fb * omega)], -1).astype(np.float32)
    return np.pad(z, ((0, 0), (0, LANES - HYENA_EMB)))


def hyena_mixer(u, p, mats):
    b, n, _ = u.shape
    cmat, sfwd, sinv = mats
    xs = short_conv(u, p['short_conv_w'])
    x1, x2, v = xs[0], xs[1], xs[2]
    deltas = np.abs(np.linspace(HYENA_MIN_DECAY, HYENA_MAX_DECAY, D_HYENA, dtype=np.float32))[None, :]
    w1p = jnp.pad(p['filt_w1'], ((0, LANES - HYENA_EMB), (0, 0)))
    h, nyq = hyena_filters(n, jnp.asarray(_hyena_features(n)), w1p, p['filt_b1'][None, :],
                           p['filt_w2'], p['filt_b2'][None, :], p['filt_w3'],
                           p['filt_freq'][None, :], jnp.asarray(deltas))
    kre, kim = filter_spectrum(cmat, sfwd, h)
    kim = kim.at[0].set(nyq.reshape(-1))
    z = v.astype(BF16)
    for o, gate in enumerate((x1, x2)):
        yre, yim = spectral_multiply(cmat, sfwd, z, kre, kim, o)
        z = inverse_gate(cmat, sinv, yre, yim, z, gate, p['hyena_bias'][o][None, :])
    return z


def _fnet_pq_kernel(x_ref, m_ref, p_ref, q_ref):
    pq = jnp.dot(x_ref[...], m_ref[...].astype(BF16), preferred_element_type=F32)
    p_ref[...] = pq[:, :D_FNET].astype(BF16)
    q_ref[...] = pq[:, D_FNET:].astype(BF16)


def _fnet_dft_kernel(c_ref, s_ref, p_ref, q_ref, o_ref):
    o_ref[...] = (jnp.dot(c_ref[...], p_ref[...], preferred_element_type=F32)
                  + jnp.dot(s_ref[...], q_ref[...], preferred_element_type=F32)).astype(o_ref.dtype)


def fourier_mixer(fr, fnet_w, mats):
    b, n, _ = fr.shape
    cmat, smat = mats
    cw = np.arange(FNET_GROUP_W)
    angw = 2.0 * math.pi * ((cw[:, None] * cw[None, :]) % FNET_GROUP_W) / FNET_GROUP_W
    scale = 1.0 / math.sqrt(n * FNET_GROUP_W)
    eye = np.eye(FNET_GROUPS)
    cbd = np.kron(eye, np.cos(angw) * scale).astype(np.float32)
    sbd = np.kron(eye, np.sin(angw) * scale).astype(np.float32)
    wbd = jnp.zeros((D_FNET, D_FNET), F32)
    for g in range(FNET_GROUPS):
        sl = slice(g * FNET_GROUP_W, (g + 1) * FNET_GROUP_W)
        wbd = wbd.at[sl, sl].set(fnet_w[g])
    mcs = small_matmul(jnp.asarray(np.concatenate([cbd, sbd], 0)), wbd)
    m = jnp.concatenate([mcs[:D_FNET], mcs[D_FNET:]], axis=1)
    tm = min(DFT_TILE, n)
    pm, qm = pl.pallas_call(
        _fnet_pq_kernel,
        grid=(b, n // tm),
        in_specs=[pl.BlockSpec((None, tm, D_FNET), lambda i, j: (i, j, 0)),
                  pl.BlockSpec((D_FNET, 2 * D_FNET), lambda i, j: (0, 0))],
        out_specs=[pl.BlockSpec((None, tm, D_FNET), lambda i, j: (i, j, 0))] * 2,
        out_shape=[jax.ShapeDtypeStruct((b, n, D_FNET), BF16)] * 2,
        compiler_params=_cparams(("arbitrary", "arbitrary")),
        name="fnet_channel_dft",
    )(fr, m)
    return pl.pallas_call(
        _fnet_dft_kernel,
        grid=(n // tm, b),
        in_specs=[pl.BlockSpec((tm, n), lambda i, j: (i, 0)),
                  pl.BlockSpec((tm, n), lambda i, j: (i, 0)),
                  pl.BlockSpec((None, n, D_FNET), lambda i, j: (j, 0, 0)),
                  pl.BlockSpec((None, n, D_FNET), lambda i, j: (j, 0, 0))],
        out_specs=pl.BlockSpec((None, tm, D_FNET), lambda i, j: (j, i, 0)),
        out_shape=jax.ShapeDtypeStruct((b, n, D_FNET), BF16),
        compiler_params=_cparams(("arbitrary", "arbitrary")),
        name="fnet_sequence_dft",
    )(cmat, smat, pm, qm)


def _outproj_kernel(sid_ref, hy_ref, at_ref, fn_ref, w_ref, x_ref, mod_ref, g_ref, b_ref, rw_ref,
                    x1_ref, h2_ref, lg_ref):
    del sid_ref
    mix = (jnp.dot(hy_ref[...], w_ref[0:D_HYENA, :], preferred_element_type=F32)
           + jnp.dot(at_ref[...], w_ref[D_HYENA:D_HYENA + D_DIFF, :], preferred_element_type=F32)
           + jnp.dot(fn_ref[...], w_ref[D_HYENA + D_DIFF:, :], preferred_element_type=F32))
    g1 = mod_ref[:, 2 * D_MODEL:3 * D_MODEL]
    sh2 = mod_ref[:, 3 * D_MODEL:4 * D_MODEL]
    sc2 = mod_ref[:, 4 * D_MODEL:5 * D_MODEL]
    x1 = _ln(ALPHA * x_ref[...] + g1 * mix) * g_ref[...] + b_ref[...]
    x1_ref[...] = x1
    h2 = _ln(x1) * (1.0 + sc2) + sh2
    h2_ref[...] = h2
    lg_ref[...] = lax.dot_general(rw_ref[...], h2, (((1,), (1,)), ((), ())),
                                  precision=HIGHEST, preferred_element_type=F32)


def out_projection(hy, at, fn, w_out_bf16, x, mod, ln_g, ln_b, router_wt, sid):
    t = x.shape[0]
    tm = ROW_TILE
    row = lambda w: pl.BlockSpec((tm, w), lambda i, sid: (i, 0))
    grid_spec = pltpu.PrefetchScalarGridSpec(
        num_scalar_prefetch=1,
        grid=(t // tm,),
        in_specs=[row(D_HYENA), row(D_DIFF), row(D_FNET),
                  pl.BlockSpec((D_MIX, D_MODEL), lambda i, sid: (0, 0)),
                  row(D_MODEL),
                  pl.BlockSpec((None, 1, 6 * D_MODEL), lambda i, sid: (sid[i], 0, 0)),
                  pl.BlockSpec((1, D_MODEL), lambda i, sid: (0, 0)),
                  pl.BlockSpec((1, D_MODEL), lambda i, sid: (0, 0)),
                  pl.BlockSpec((N_EXPERTS, D_MODEL), lambda i, sid: (0, 0))],
        out_specs=[row(D_MODEL), row(D_MODEL),
                   pl.BlockSpec((N_EXPERTS, tm), lambda i, sid: (0, i))],
    )
    return pl.pallas_call(
        _outproj_kernel,
        grid_spec=grid_spec,
        out_shape=[jax.ShapeDtypeStruct((t, D_MODEL), F32),
                   jax.ShapeDtypeStruct((t, D_MODEL), F32),
                   jax.ShapeDtypeStruct((N_EXPERTS, t), F32)],
        compiler_params=_cparams(("arbitrary",)),
        name="out_projection",
    )(sid, hy, at, fn, w_out_bf16, x, mod, ln_g, ln_b, router_wt)


def _route_kernel(lg_ref, bias_ref, tri_ref, eidx_ref, gate_ref, rank_ref, cnt_ref, carry_ref):
    @pl.when(pl.program_id(0) == 0)
    def _():
        carry_ref[...] = jnp.zeros_like(carry_ref)

    gsz = N_EXPERTS // N_GROUPS
    neg = -jnp.inf
    s = 1.0 / (1.0 + jnp.exp(-lg_ref[...]))
    sel = s + bias_ref[...]
    tt = s.shape[1]
    sub = lax.broadcasted_iota(jnp.int32, (gsz, tt), 0)
    gs = []
    for g in range(N_GROUPS):
        blk = sel[g * gsz:(g + 1) * gsz, :]
        m1 = blk.max(0, keepdims=True)
        first = jnp.min(jnp.where(blk == m1, sub, gsz), 0, keepdims=True)
        m2 = jnp.where(sub == first, neg, blk).max(0, keepdims=True)
        gs.append(m1 + m2)
    gscore = jnp.concatenate(gs, 0)
    gi = lax.broadcasted_iota(jnp.int32, (N_GROUPS, tt), 0)
    gkeep = jnp.zeros((N_GROUPS, tt), F32)
    cur = gscore
    for _ in range(TOPK_GROUPS):
        mx = cur.max(0, keepdims=True)
        idx = jnp.min(jnp.where(cur == mx, gi, N_GROUPS), 0, keepdims=True)
        pick = gi == idx
        gkeep = jnp.where(pick, 1.0, gkeep)
        cur = jnp.where(pick, neg, cur)
    keep = jnp.concatenate(
        [jnp.broadcast_to(gkeep[g:g + 1, :], (gsz, tt)) for g in range(N_GROUPS)], 0)
    cur = jnp.where(keep > 0.5, sel, neg)
    ei = lax.broadcasted_iota(jnp.int32, (N_EXPERTS, tt), 0)
    picks, idxs, ws = [], [], []
    onehot = jnp.zeros((N_EXPERTS, tt), F32)
    for _ in range(TOP_K):
        mx = cur.max(0, keepdims=True)
        idx = jnp.min(jnp.where(cur == mx, ei, N_EXPERTS), 0, keepdims=True)
        pick = ei == idx
        picks.append(pick)
        idxs.append(idx)
        ws.append(jnp.sum(jnp.where(pick, s, 0.0), 0, keepdims=True))
        cur = jnp.where(pick, neg, cur)
        onehot = jnp.where(pick, 1.0, onehot)
    wsum = ws[0]
    for w in ws[1:]:
        wsum = wsum + w
    inv = ROUTED_SCALE / wsum
    prefix = jnp.dot(onehot.astype(BF16), tri_ref[...], preferred_element_type=F32)
    rank_e = carry_ref[...] + prefix
    eidx_ref[...] = jnp.concatenate(idxs, 0)
    gate_ref[...] = jnp.concatenate([w * inv for w in ws], 0)
    rank_ref[...] = jnp.concatenate(
        [jnp.sum(jnp.where(pk, rank_e, 0.0), 0, keepdims=True) for pk in picks], 0).astype(jnp.int32)
    carry_ref[...] = carry_ref[...] + jnp.sum(onehot, 1, keepdims=True)
    cnt_ref[...] = carry_ref[...]


def route(logits_t, router_b):
    t = logits_t.shape[1]
    tt = ROUTE_TILE
    tri = jnp.asarray(np.triu(np.ones((tt, tt), np.float32), 1), BF16)
    col = lambda: pl.BlockSpec((TOP_K, tt), lambda i: (0, i))
    return pl.pallas_call(
        _route_kernel,
        grid=(t // tt,),
        in_specs=[pl.BlockSpec((N_EXPERTS, tt), lambda i: (0, i)),
                  pl.BlockSpec((N_EXPERTS, 1), lambda i: (0, 0)),
                  pl.BlockSpec((tt, tt), lambda i: (0, 0))],
        out_specs=[col(), col(), col(), pl.BlockSpec((N_EXPERTS, 1), lambda i: (0, 0))],
        out_shape=[jax.ShapeDtypeStruct((TOP_K, t), jnp.int32),
                   jax.ShapeDtypeStruct((TOP_K, t), F32),
                   jax.ShapeDtypeStruct((TOP_K, t), jnp.int32),
                   jax.ShapeDtypeStruct((N_EXPERTS, 1), F32)],
        scratch_shapes=[pltpu.VMEM((N_EXPERTS, 1), F32)],
        compiler_params=_cparams(("arbitrary",)),
        name="route",
    )(logits_t, router_b.reshape(N_EXPERTS, 1), tri)


def _row_copy(src, dst, sem):
    return pltpu.make_async_copy(src, dst, sem)


def _dispatch_kernel(pos_ref, h_ref, xg_in_ref, xg_ref, sem):
    del xg_in_ref
    tm = h_ref.shape[0]

    def issue(t, carry):
        for k in range(TOP_K):
            p = pos_ref[t * TOP_K + k]
            _row_copy(h_ref.at[pl.ds(t, 1), :], xg_ref.at[pl.ds(p, 1), :], sem).start(priority=k % 2)
        return carry

    lax.fori_loop(0, tm, issue, 0)

    def drain(t, carry):
        for k in range(TOP_K):
            _row_copy(h_ref.at[pl.ds(0, 1), :], xg_ref.at[pl.ds(0, 1), :], sem).wait()
        return carry

    lax.fori_loop(0, tm, drain, 0)


def dispatch(pos_flat, h2, xg):
    t = h2.shape[0]
    tm = ROW_TILE
    return pl.pallas_call(
        _dispatch_kernel,
        grid=(t // tm,),
        in_specs=[pl.BlockSpec((tm * TOP_K,), lambda i: (i,), memory_space=pltpu.SMEM),
                  pl.BlockSpec((tm, D_MODEL), lambda i: (i, 0)),
                  pl.BlockSpec(memory_space=pl.ANY)],
        out_specs=pl.BlockSpec(memory_space=pl.ANY),
        out_shape=jax.ShapeDtypeStruct(xg.shape, xg.dtype),
        scratch_shapes=[pltpu.SemaphoreType.DMA(())],
        input_output_aliases={2: 0},
        compiler_params=_cparams(("arbitrary",)),
        name="moe_dispatch",
    )(pos_flat, h2, xg)


def _expert_kernel(be_ref, nu_ref, x_ref, wg_ref, wu_ref, wd_ref, y_ref):
    del be_ref

    @pl.when(pl.program_id(0) < nu_ref[0])
    def _():
        x = x_ref[...].astype(BF16)
        a = (_silu(jnp.dot(x, wg_ref[...], preferred_element_type=F32))
             * jnp.dot(x, wu_ref[...], preferred_element_type=F32))
        y_ref[...] = jnp.dot(a.astype(BF16), wd_ref[...], preferred_element_type=F32)

    @pl.when(pl.program_id(0) >= nu_ref[0])
    def _():
        y_ref[...] = jnp.zeros_like(y_ref)


def expert_ffn(block_exp, n_used, xg, wg, wu, wd):
    p = xg.shape[0]
    blk = MOE_BLK
    grid_spec = pltpu.PrefetchScalarGridSpec(
        num_scalar_prefetch=2,
        grid=(p // blk,),
        in_specs=[pl.BlockSpec((blk, D_MODEL), lambda i, be, nu: (i, 0)),
                  pl.BlockSpec((None, D_MODEL, D_EXPERT), lambda i, be, nu: (be[i], 0, 0)),
                  pl.BlockSpec((None, D_MODEL, D_EXPERT), lambda i, be, nu: (be[i], 0, 0)),
                  pl.BlockSpec((None, D_EXPERT, D_MODEL), lambda i, be, nu: (be[i], 0, 0))],
        out_specs=pl.BlockSpec((blk, D_MODEL), lambda i, be, nu: (i, 0)),
    )
    return pl.pallas_call(
        _expert_kernel,
        grid_spec=grid_spec,
        out_shape=jax.ShapeDtypeStruct((p, D_MODEL), F32),
        compiler_params=_cparams(("arbitrary",)),
        name="expert_ffn",
    )(block_exp, n_used, xg, wg, wu, wd)


def _combine_kernel(sid_ref, pos_ref, yb_ref, gate_ref, h_ref, sg_ref, su_ref, sd_ref, x_ref,
                    mod_ref, g_ref, b_ref, o_ref, buf_ref, sem):
    del sid_ref
    tm = h_ref.shape[0]

    def issue(t, carry):
        for k in range(TOP_K):
            p = pos_ref[t * TOP_K + k]
            _row_copy(yb_ref.at[pl.ds(p, 1), :], buf_ref.at[k, pl.ds(t, 1), :], sem).start(priority=k % 2)
        return carry

    lax.fori_loop(0, tm, issue, 0)
    h = h_ref[...].astype(BF16)
    a = (_silu(jnp.dot(h, sg_ref[...], preferred_element_type=F32))
         * jnp.dot(h, su_ref[...], preferred_element_type=F32))
    y = jnp.dot(a.astype(BF16), sd_ref[...], preferred_element_type=F32)

    def drain(t, carry):
        for k in range(TOP_K):
            _row_copy(yb_ref.at[pl.ds(0, 1), :], buf_ref.at[0, pl.ds(0, 1), :], sem).wait()
        return carry

    lax.fori_loop(0, tm, drain, 0)
    gates = gate_ref[...]
    for k in range(TOP_K):
        y = y + buf_ref[k] * gates[:, k:k + 1]
    g2 = mod_ref[:, 5 * D_MODEL:6 * D_MODEL]
    o_ref[...] = _ln(ALPHA * x_ref[...] + g2 * y) * g_ref[...] + b_ref[...]


def combine(sid, pos_flat, yb, gates_tk, h2, sg, su, sd, x1, mod, ln_g, ln_b):
    t = h2.shape[0]
    tm = COMBINE_TILE
    row = lambda w: pl.BlockSpec((tm, w), lambda i, sid: (i, 0))
    const = lambda shape: pl.BlockSpec(shape, lambda i, sid: (0,) * len(shape))
    grid_spec = pltpu.PrefetchScalarGridSpec(
        num_scalar_prefetch=1,
        grid=(t // tm,),
        in_specs=[pl.BlockSpec((tm * TOP_K,), lambda i, sid: (i,), memory_space=pltpu.SMEM),
                  pl.BlockSpec(memory_space=pl.ANY),
                  row(TOP_K), row(D_MODEL),
                  const((D_MODEL, D_SHARED)), const((D_MODEL, D_SHARED)), const((D_SHARED, D_MODEL)),
                  row(D_MODEL),
                  pl.BlockSpec((None, 1, 6 * D_MODEL), lambda i, sid: (sid[i], 0, 0)),
                  const((1, D_MODEL)), const((1, D_MODEL))],
        out_specs=row(D_MODEL),
        scratch_shapes=[pltpu.VMEM((TOP_K, tm, D_MODEL), F32), pltpu.SemaphoreType.DMA(())],
    )
    return pl.pallas_call(
        _combine_kernel,
        grid_spec=grid_spec,
        out_shape=jax.ShapeDtypeStruct((t, D_MODEL), F32),
        compiler_params=_cparams(("arbitrary",)),
        name="moe_combine",
    )(sid, pos_flat, yb, gates_tk, h2, sg, su, sd, x1, mod, ln_g, ln_b)


def _moe_padded_rows(t):
    a = t * TOP_K
    n_blocks = (a + N_EXPERTS * (MOE_BLK - 1) + MOE_BLK - 1) // MOE_BLK
    return n_blocks, n_blocks * MOE_BLK


def moe_slots(eidx, rank, counts, t):
    n_blocks, _ = _moe_padded_rows(t)
    cnt = counts.reshape(-1).astype(jnp.int32)
    padded = (cnt + MOE_BLK - 1) // MOE_BLK * MOE_BLK
    cum_pad = jnp.cumsum(padded)
    pstart = cum_pad - padded
    experts = jnp.arange(N_EXPERTS, dtype=jnp.int32)
    pos = jnp.sum(jnp.where(eidx[None] == experts[:, None, None], pstart[:, None, None], 0), 0) + rank
    starts = jnp.arange(n_blocks, dtype=jnp.int32) * MOE_BLK
    block_exp = jnp.minimum(jnp.sum((cum_pad[None, :] <= starts[:, None]).astype(jnp.int32), 1),
                            N_EXPERTS - 1)
    n_used = (cum_pad[-1] // MOE_BLK).astype(jnp.int32).reshape(1)
    return pos.T.reshape(-1), block_exp, n_used


def _rope_tables():
    n = DIFF_QK // 4
    inv = 1.0 / (ROPE_BASE ** (np.arange(n, dtype=np.float32) / n))
    t = np.arange(SEQ)
    row = (t // GRID_W).astype(np.float32)
    col = (t % GRID_W).astype(np.float32)
    ang = np.stack([row[:, None] * inv, col[:, None] * inv], axis=1).astype(np.float32)
    cos = np.repeat(np.cos(ang).reshape(SEQ, 2 * n), 2, axis=1)
    sin = np.repeat(np.sin(ang).reshape(SEQ, 2 * n), 2, axis=1)
    sign = np.tile(np.array([-1.0, 1.0], np.float32), DIFF_QK // 2)
    sin = sin * sign[None, :]
    cos = np.tile(cos, (1, LANES // DIFF_QK))
    sin = np.tile(sin, (1, LANES // DIFF_QK))
    cos = np.concatenate([cos, np.ones((ROW_TILE, LANES), np.float32)], 0)
    sin = np.concatenate([sin, np.zeros((ROW_TILE, LANES), np.float32)], 0)
    return jnp.asarray(cos, F32), jnp.asarray(sin, F32)


def _tile_ids(t, tile):
    n = t // tile
    i = np.arange(n)
    lat = T_LAT // tile
    sid = np.where(i < lat, i // (SEQ // tile), BATCH).astype(np.int32)
    rid = np.where(i < lat, i % (SEQ // tile), SEQ // tile).astype(np.int32)
    return jnp.asarray(sid), jnp.asarray(rid)


def kernel(x, c, ctx, c_ctx, w_mod, b_mod, w_in, w_out, short_conv_w, filt_w1, filt_b1, filt_w2,
           filt_b2, filt_w3, filt_freq, hyena_bias, lam_q1, lam_k1, lam_q2, lam_k2, subln_w, fnet_w,
           ln1_g, ln1_b, ln2_g, ln2_b, router_w, router_b, exp_w_gate, exp_w_up, exp_w_down,
           sh_w_gate, sh_w_up, sh_w_down):
    xt = jnp.concatenate([x.reshape(T_LAT, D_MODEL), ctx.reshape(T_CTX, D_MODEL)], 0)
    cc = jnp.concatenate([c, c_ctx[None, :], jnp.zeros((3, D_MODEL), F32)], 0)
    cos_t, sin_t = _rope_tables()
    sid_all, rid_all = _tile_ids(T_ALL, ROW_TILE)
    hy_mats = {SEQ: _hyena_dft_mats(SEQ), CTX_LEN: _hyena_dft_mats(CTX_LEN)}
    fn_mats = {SEQ: _fnet_dft_mats(SEQ), CTX_LEN: _fnet_dft_mats(CTX_LEN)}
    _, p_rows = _moe_padded_rows(T_ALL)
    xg = jnp.zeros((p_rows, D_MODEL), F32)

    for li in range(DEPTH):
        last = li == DEPTH - 1
        lam_init = 0.8 - 0.6 * math.exp(-0.3 * li)
        p = {'short_conv_w': short_conv_w[li], 'filt_w1': filt_w1[li], 'filt_b1': filt_b1[li],
             'filt_w2': filt_w2[li], 'filt_b2': filt_b2[li], 'filt_w3': filt_w3[li],
             'filt_freq': filt_freq[li], 'hyena_bias': hyena_bias[li]}
        mod = small_matmul(cc, w_mod[li], b_mod[li][None, :], silu_in=True, tn=6 * D_MODEL // 4)
        mod = mod.reshape(8, 1, 6 * D_MODEL)
        lam = (jnp.exp(jnp.sum(lam_q1[li] * lam_k1[li])) - jnp.exp(jnp.sum(lam_q2[li] * lam_k2[li]))
               + lam_init).reshape(1).astype(F32)

        hy, q, k, v, fr = in_projection(xt, mod, w_in[li].astype(BF16), cos_t, sin_t, sid_all, rid_all)
        subln2 = jnp.tile(subln_w[li], 2)[None, :]
        t_act = T_LAT if last else T_ALL
        att = diff_attention(lam, q, k, v, subln2, lam_init=lam_init, latent=True)
        hyo = hyena_mixer(hy[:T_LAT].reshape(BATCH, SEQ, HY_W), p, hy_mats[SEQ]).reshape(T_LAT, D_HYENA)
        fno = fourier_mixer(fr[:T_LAT].reshape(BATCH, SEQ, D_FNET), fnet_w[li],
                            fn_mats[SEQ]).reshape(T_LAT, D_FNET)
        if not last:
            att_c = diff_attention(lam, q, k, v, subln2, lam_init=lam_init, latent=False)
            hyo_c = hyena_mixer(hy[T_LAT:].reshape(BATCH, CTX_LEN, HY_W), p, hy_mats[CTX_LEN])
            fno_c = fourier_mixer(fr[T_LAT:].reshape(BATCH, CTX_LEN, D_FNET), fnet_w[li],
                                  fn_mats[CTX_LEN])
            att = jnp.concatenate([att, att_c], 0)
            hyo = jnp.concatenate([hyo, hyo_c.reshape(T_CTX, D_HYENA)], 0)
            fno = jnp.concatenate([fno, fno_c.reshape(T_CTX, D_FNET)], 0)

        sid_r, _ = _tile_ids(t_act, ROW_TILE)
        x1, h2, logits_t = out_projection(
            hyo, att, fno, w_out[li].astype(BF16), xt[:t_act], mod,
            ln1_g[li][None, :], ln1_b[li][None, :], router_w[li].T, sid_r)

        eidx, gates, rank, counts = route(logits_t, router_b[li])
        pos_flat, block_exp, n_used = moe_slots(eidx, rank, counts, T_ALL)
        xg = dispatch(pos_flat, h2, xg)
        yb = expert_ffn(block_exp, n_used, xg, exp_w_gate[li].astype(BF16),
                        exp_w_up[li].astype(BF16), exp_w_down[li].astype(BF16))
        sid_c, _ = _tile_ids(t_act, COMBINE_TILE)
        xt = combine(sid_c, pos_flat, yb, gates.T, h2, sh_w_gate[li].astype(BF16),
                     sh_w_up[li].astype(BF16), sh_w_down[li].astype(BF16), x1, mod,
                     ln2_g[li][None, :], ln2_b[li][None, :])
    return xt[:T_LAT].reshape(BATCH, SEQ, D_MODEL)
```

```python
import functools
import math

import jax
import jax.numpy as jnp
import numpy as np
from jax import lax
from jax.experimental import pallas as pl
from jax.experimental.pallas import tpu as pltpu

F32 = jnp.float32
BF16 = jnp.bfloat16
HIGHEST = lax.Precision.HIGHEST

D_MODEL = 1024
BATCH = 4
SEQ = 4096
DEPTH = 4
GRID_W = 64
CTX_LEN = 256
D_HYENA = 256
HYENA_ORDER = 2
HYENA_EMB = 33
HYENA_FILTER_WIDTH = 64
HYENA_TARGET = 1e-2
HYENA_MIN_DECAY = math.log(HYENA_TARGET) / 1.5
HYENA_MAX_DECAY = math.log(HYENA_TARGET) / 0.3
SHORT_CONV = 3
N_DIFF_HEADS = 8
DIFF_QK = 32
DIFF_V = 64
D_DIFF = N_DIFF_HEADS * DIFF_V
D_FNET = 256
FNET_GROUPS = 4
FNET_GROUP_W = D_FNET // FNET_GROUPS
D_MIX = D_HYENA + D_DIFF + D_FNET
ROPE_BASE = 10000.0
HY_W = 3 * D_HYENA
QK_W = N_DIFF_HEADS * 2 * DIFF_QK
Q_OFF = HY_W
K_OFF = Q_OFF + QK_W
V_OFF = K_OFF + QK_W
F_OFF = V_OFF + D_DIFF
D_IN = F_OFF + D_FNET
N_EXPERTS = 64
TOP_K = 8
N_GROUPS = 8
TOPK_GROUPS = 4
D_EXPERT = 256
D_SHARED = 256
ROUTED_SCALE = 2.5
ALPHA = (2 * DEPTH) ** 0.25
LN_EPS = 1e-5

T_LAT = BATCH * SEQ
T_CTX = BATCH * CTX_LEN
T_ALL = T_LAT + T_CTX

LANES = 128
VMEM_LIMIT = 56 * 1024 * 1024
ROW_TILE = 256
ATTN_TQ = 256
DFT_TILE = 512
ROUTE_TILE = 512
MOE_BLK = 256
COMBINE_TILE = 128
LOG2E = 1.4426950408889634


def _cparams(sem):
    return pltpu.CompilerParams(dimension_semantics=sem, vmem_limit_bytes=VMEM_LIMIT)


def _ln(x):
    mu = jnp.mean(x, -1, keepdims=True)
    xc = x - mu
    var = jnp.mean(xc * xc, -1, keepdims=True)
    return xc * lax.rsqrt(var + LN_EPS)


def _silu(x):
    return x * (1.0 / (1.0 + jnp.exp(-x)))


def _small_mm_kernel(a_ref, w_ref, b_ref, o_ref, *, silu_in):
    a = a_ref[...]
    if silu_in:
        a = _silu(a)
    o_ref[...] = jnp.dot(a, w_ref[...], precision=HIGHEST, preferred_element_type=F32) + b_ref[...]


def small_matmul(a, w, b=None, *, silu_in=False, tn=None):
    m, k = a.shape
    n = w.shape[1]
    tn = n if tn is None else tn
    if b is None:
        b = jnp.zeros((1, n), F32)
    return pl.pallas_call(
        functools.partial(_small_mm_kernel, silu_in=silu_in),
        grid=(n // tn,),
        in_specs=[pl.BlockSpec((m, k), lambda j: (0, 0)),
                  pl.BlockSpec((k, tn), lambda j: (0, j)),
                  pl.BlockSpec((1, tn), lambda j: (0, j))],
        out_specs=pl.BlockSpec((m, tn), lambda j: (0, j)),
        out_shape=jax.ShapeDtypeStruct((m, n), F32),
        compiler_params=_cparams(("arbitrary",)),
        name="small_matmul",
    )(a, w, b)


def _inproj_kernel(sid_ref, rid_ref, x_ref, mod_ref, w_ref, cos_ref, sin_ref,
                   hy_ref, q_ref, k_ref, v_ref, fr_ref):
    del sid_ref, rid_ref
    x = x_ref[...]
    sh = mod_ref[:, 0:D_MODEL]
    sc = mod_ref[:, D_MODEL:2 * D_MODEL]
    h = (_ln(x) * (1.0 + sc) + sh).astype(BF16)
    z = jnp.dot(h, w_ref[...], preferred_element_type=F32)
    hy_ref[...] = z[:, :HY_W]
    cos = jnp.concatenate([cos_ref[...]] * (QK_W // LANES), axis=1)
    sin = jnp.concatenate([sin_ref[...]] * (QK_W // LANES), axis=1)
    lane = lax.broadcasted_iota(jnp.int32, (1, QK_W), 1)
    even = (lane % 2) == 0

    def rope(t):
        nxt = pltpu.roll(t, QK_W - 1, 1)
        prv = pltpu.roll(t, 1, 1)
        return t * cos + jnp.where(even, nxt, prv) * sin

    qscale = (DIFF_QK ** -0.5) * LOG2E
    q_ref[...] = (rope(z[:, Q_OFF:K_OFF]) * qscale).astype(BF16)
    k_ref[...] = rope(z[:, K_OFF:V_OFF]).astype(BF16)
    v_ref[...] = z[:, V_OFF:F_OFF].astype(BF16)
    fr_ref[...] = z[:, F_OFF:].astype(BF16)


def in_projection(x, mod, w_in_bf16, cos_t, sin_t, sid, rid):
    t = x.shape[0]
    tm = ROW_TILE
    grid_spec = pltpu.PrefetchScalarGridSpec(
        num_scalar_prefetch=2,
        grid=(t // tm,),
        in_specs=[
            pl.BlockSpec((tm, D_MODEL), lambda i, sid, rid: (i, 0)),
            pl.BlockSpec((None, 1, 6 * D_MODEL), lambda i, sid, rid: (sid[i], 0, 0)),
            pl.BlockSpec((D_MODEL, D_IN), lambda i, sid, rid: (0, 0)),
            pl.BlockSpec((tm, LANES), lambda i, sid, rid: (rid[i], 0)),
            pl.BlockSpec((tm, LANES), lambda i, sid, rid: (rid[i], 0)),
        ],
        out_specs=[
            pl.BlockSpec((tm, HY_W), lambda i, sid, rid: (i, 0)),
            pl.BlockSpec((tm, QK_W), lambda i, sid, rid: (i, 0)),
            pl.BlockSpec((tm, QK_W), lambda i, sid, rid: (i, 0)),
            pl.BlockSpec((tm, D_DIFF), lambda i, sid, rid: (i, 0)),
            pl.BlockSpec((tm, D_FNET), lambda i, sid, rid: (i, 0)),
        ],
    )
    return pl.pallas_call(
        _inproj_kernel,
        grid_spec=grid_spec,
        out_shape=[jax.ShapeDtypeStruct((t, HY_W), F32),
                   jax.ShapeDtypeStruct((t, QK_W), BF16),
                   jax.ShapeDtypeStruct((t, QK_W), BF16),
                   jax.ShapeDtypeStruct((t, D_DIFF), BF16),
                   jax.ShapeDtypeStruct((t, D_FNET), BF16)],
        compiler_params=_cparams(("arbitrary",)),
        name="in_projection",
    )(sid, rid, x, mod, w_in_bf16, cos_t, sin_t)


def _attn_kernel(lam_ref, q_ref, *refs, lam_init, n_seg):
    k_refs = refs[:n_seg]
    v_refs = refs[n_seg:2 * n_seg]
    w_ref = refs[2 * n_seg]
    o_ref = refs[2 * n_seg + 1]
    q = q_ref[...]
    lane = lax.broadcasted_iota(jnp.int32, (1, LANES), 1)
    lam = lam_ref[0]
    nt = (((1,), (1,)), ((), ()))
    first = lane < DIFF_V
    scores = []
    for u in range(4):
        lo = (u // 2) * DIFF_V + (u % 2) * DIFF_QK
        qm = jnp.where((lane >= lo) & (lane < lo + DIFF_QK), q, jnp.zeros_like(q))
        scores.append([lax.dot_general(qm, kr[...], nt, preferred_element_type=F32) for kr in k_refs])
    probs = []
    for s in scores:
        mx = s[0].max(-1, keepdims=True)
        for sp in s[1:]:
            mx = jnp.maximum(mx, sp.max(-1, keepdims=True))
        e = [jnp.exp2(sp - mx) for sp in s]
        den = e[0].sum(-1, keepdims=True)
        for ep in e[1:]:
            den = den + ep.sum(-1, keepdims=True)
        probs.append((e, 1.0 / den))
    heads = []
    for h in range(2):
        (e0, r0), (e1, r1) = probs[2 * h], probs[2 * h + 1]
        r1 = lam * r1
        o = None
        for seg in range(n_seg):
            w = (e0[seg] * r0 - e1[seg] * r1).astype(BF16)
            part = jnp.dot(w, v_refs[seg][...], preferred_element_type=F32)
            o = part if o is None else o + part
        heads.append(o)
    o = jnp.where(first, heads[0], heads[1])
    sq = o * o
    ms0 = jnp.sum(jnp.where(first, sq, 0.0), -1, keepdims=True)
    ms1 = jnp.sum(jnp.where(first, 0.0, sq), -1, keepdims=True)
    ms = jnp.where(first, ms0, ms1) * (1.0 / DIFF_V)
    y = o * lax.rsqrt(ms + LN_EPS) * w_ref[...] * (1.0 - lam_init)
    o_ref[...] = y.astype(o_ref.dtype)


def diff_attention(lam, q, k, v, subln2, *, lam_init, latent):
    tq = ATTN_TQ
    n_q = (SEQ if latent else CTX_LEN) // tq
    q_base = 0 if latent else T_LAT // tq
    ctx_blk = T_LAT // CTX_LEN

    specs, args = [], []
    for arr in (k, v):
        if latent:
            specs.append(pl.BlockSpec((SEQ, LANES), lambda b, hp, i: (b, hp)))
            args.append(arr)
        specs.append(pl.BlockSpec((CTX_LEN, LANES), lambda b, hp, i: (ctx_blk + b, hp)))
        args.append(arr)
    n_seg = 2 if latent else 1
    rows = T_LAT if latent else T_CTX
    return pl.pallas_call(
        functools.partial(_attn_kernel, lam_init=lam_init, n_seg=n_seg),
        grid=(BATCH, N_DIFF_HEADS // 2, n_q),
        in_specs=[pl.BlockSpec(memory_space=pltpu.SMEM),
                  pl.BlockSpec((tq, LANES), lambda b, hp, i: (q_base + b * n_q + i, hp))] + specs
                 + [pl.BlockSpec((1, LANES), lambda b, hp, i: (0, 0))],
        out_specs=pl.BlockSpec((tq, LANES), lambda b, hp, i: (b * n_q + i, hp)),
        out_shape=jax.ShapeDtypeStruct((rows, D_DIFF), BF16),
        compiler_params=_cparams(("arbitrary", "arbitrary", "arbitrary")),
        name="diff_attention_lat" if latent else "diff_attention_ctx",
    )(lam, q, *args, subln2)


def _short_conv_kernel(u_ref, w_ref, o_ref):
    u = u_ref[...]
    n = u.shape[0]
    row = lax.broadcasted_iota(jnp.int32, (n, 1), 0)
    prv = jnp.where(row == 0, 0.0, pltpu.roll(u, 1, 0))
    nxt = jnp.where(row == n - 1, 0.0, pltpu.roll(u, n - 1, 0))
    o_ref[...] = prv * w_ref[0:1, :] + u * w_ref[1:2, :] + nxt * w_ref[2:3, :]


def short_conv(u, w):
    b, n, _ = u.shape
    return pl.pallas_call(
        _short_conv_kernel,
        grid=(b, 3),
        in_specs=[pl.BlockSpec((None, n, D_HYENA), lambda i, j: (i, 0, j)),
                  pl.BlockSpec((SHORT_CONV, D_HYENA), lambda i, j: (0, j))],
        out_specs=pl.BlockSpec((None, None, n, D_HYENA), lambda i, j: (j, i, 0, 0)),
        out_shape=jax.ShapeDtypeStruct((3, b, n, D_HYENA), F32),
        compiler_params=_cparams(("arbitrary", "arbitrary")),
        name="short_conv",
    )(u, w)


def _filter_kernel(z_ref, w1_ref, b1_ref, w2_ref, b2_ref, w3_ref, fq_ref, dl_ref, h_ref, ny_ref):
    n = z_ref.shape[0]
    fq = fq_ref[...]
    a = jnp.sin(fq * (jnp.dot(z_ref[...], w1_ref[...], precision=HIGHEST,
                              preferred_element_type=F32) + b1_ref[...]))
    a = jnp.sin(fq * (jnp.dot(a, w2_ref[...], precision=HIGHEST,
                              preferred_element_type=F32) + b2_ref[...]))
    row = lax.broadcasted_iota(jnp.int32, (n, 1), 0)
    t = row.astype(F32) * (1.0 / (n - 1))
    win = jnp.exp(-t * dl_ref[...])
    alt = jnp.where(row % 2 == 0, 1.0, -1.0)
    hf = jnp.dot(a, w3_ref[:, 0:D_HYENA], precision=HIGHEST, preferred_element_type=F32) * win
    hb = jnp.dot(a, w3_ref[:, D_HYENA:], precision=HIGHEST, preferred_element_type=F32) * win
    hb0 = jnp.where(row == 0, 0.0, hb)
    l1 = jnp.sum(jnp.abs(hf), 0, keepdims=True) + jnp.sum(jnp.abs(hb0), 0, keepdims=True)
    inv = 1.0 / l1
    hf = hf * inv
    hb0 = hb0 * inv
    ks = hf + hb0
    h_ref[:, 0:D_HYENA] = ks
    h_ref[:, D_HYENA:] = hf - hb0
    ny_ref[...] = jnp.sum(ks * alt, 0, keepdims=True)


def hyena_filters(n, zfeat, w1p, b1, w2, b2, w3, freq, deltas):
    full = lambda shape: pl.BlockSpec(shape, lambda o: (0,) * len(shape))
    return pl.pallas_call(
        _filter_kernel,
        grid=(HYENA_ORDER,),
        in_specs=[full(zfeat.shape), full(w1p.shape), full(b1.shape), full(w2.shape),
                  full(b2.shape),
                  pl.BlockSpec((HYENA_FILTER_WIDTH, 2 * D_HYENA), lambda o: (0, o)),
                  full(freq.shape), full(deltas.shape)],
        out_specs=[pl.BlockSpec((n, 2 * D_HYENA), lambda o: (0, o)),
                   pl.BlockSpec((None, 1, D_HYENA), lambda o: (o, 0, 0))],
        out_shape=[jax.ShapeDtypeStruct((n, 4 * D_HYENA), F32),
                   jax.ShapeDtypeStruct((HYENA_ORDER, 1, D_HYENA), F32)],
        compiler_params=_cparams(("arbitrary",)),
        name="hyena_filters",
    )(zfeat, w1p, b1, w2, b2, w3, freq, deltas)


def _dft_fwd_kernel(c_ref, s_ref, a_ref, b_ref, re_ref, im_ref):
    re_ref[...] = jnp.dot(c_ref[...], a_ref[...].astype(BF16), preferred_element_type=F32)
    im_ref[...] = jnp.dot(s_ref[...], b_ref[...].astype(BF16), preferred_element_type=F32)


def filter_spectrum(cmat, smat, h):
    n = cmat.shape[0]
    tf = min(DFT_TILE, n)
    return pl.pallas_call(
        _dft_fwd_kernel,
        grid=(n // tf, HYENA_ORDER),
        in_specs=[pl.BlockSpec((tf, n), lambda i, o: (i, 0)),
                  pl.BlockSpec((tf, n), lambda i, o: (i, 0)),
                  pl.BlockSpec((n, D_HYENA), lambda i, o: (0, 2 * o)),
                  pl.BlockSpec((n, D_HYENA), lambda i, o: (0, 2 * o + 1))],
        out_specs=[pl.BlockSpec((tf, D_HYENA), lambda i, o: (i, o)),
                   pl.BlockSpec((tf, D_HYENA), lambda i, o: (i, o))],
        out_shape=[jax.ShapeDtypeStruct((n, HYENA_ORDER * D_HYENA), F32)] * 2,
        compiler_params=_cparams(("arbitrary", "arbitrary")),
        name="filter_spectrum",
    )(cmat, smat, h, h)


def _spec_mul_kernel(c_ref, s_ref, z_ref, kre_ref, kim_ref, yre_ref, yim_ref, *, n):
    z = z_ref[...]
    ure = jnp.dot(c_ref[...], z, preferred_element_type=F32)
    uim = jnp.dot(s_ref[...], z, preferred_element_type=F32)
    kre = kre_ref[...]
    kim = kim_ref[...]
    tf = ure.shape[0]
    grow = pl.program_id(0) * tf + lax.broadcasted_iota(jnp.int32, (tf, 1), 0)
    dc = grow == 0
    yre = jnp.where(dc, ure * kre, ure * kre - uim * kim)
    yim = jnp.where(dc, uim * kim, ure * kim + uim * kre)
    wgt = jnp.where(dc, 1.0 / (2 * n), 2.0 / (2 * n))
    yre_ref[...] = (yre * wgt).astype(BF16)
    yim_ref[...] = (yim * wgt).astype(BF16)


def spectral_multiply(cmat, sfwd, z, kre, kim, order):
    b, n, _ = z.shape
    tf = min(DFT_TILE, n)
    return pl.pallas_call(
        functools.partial(_spec_mul_kernel, n=n),
        grid=(n // tf, b),
        in_specs=[pl.BlockSpec((tf, n), lambda i, j: (i, 0)),
                  pl.BlockSpec((tf, n), lambda i, j: (i, 0)),
                  pl.BlockSpec((None, n, D_HYENA), lambda i, j: (j, 0, 0)),
                  pl.BlockSpec((tf, D_HYENA), lambda i, j: (i, order)),
                  pl.BlockSpec((tf, D_HYENA), lambda i, j: (i, order))],
        out_specs=[pl.BlockSpec((None, tf, D_HYENA), lambda i, j: (j, i, 0)),
                   pl.BlockSpec((None, tf, D_HYENA), lambda i, j: (j, i, 0))],
        out_shape=[jax.ShapeDtypeStruct((b, n, D_HYENA), BF16)] * 2,
        compiler_params=_cparams(("arbitrary", "arbitrary")),
        name="spectral_multiply",
    )(cmat, sfwd, z, kre, kim)


def _inv_gate_kernel(c_ref, s_ref, yre_ref, yim_ref, z_ref, g_ref, bias_ref, o_ref):
    y = (jnp.dot(c_ref[...], yre_ref[...], preferred_element_type=F32)
         + jnp.dot(s_ref[...], yim_ref[...], preferred_element_type=F32))
    z = z_ref[...].astype(F32)
    o_ref[...] = (g_ref[...] * (y + z * bias_ref[...])).astype(o_ref.dtype)


def inverse_gate(cmat, sinv, yre, yim, z, gate, bias):
    b, n, _ = z.shape
    tt = min(DFT_TILE, n)
    return pl.pallas_call(
        _inv_gate_kernel,
        grid=(n // tt, b),
        in_specs=[pl.BlockSpec((tt, n), lambda i, j: (i, 0)),
                  pl.BlockSpec((tt, n), lambda i, j: (i, 0)),
                  pl.BlockSpec((None, n, D_HYENA), lambda i, j: (j, 0, 0)),
                  pl.BlockSpec((None, n, D_HYENA), lambda i, j: (j, 0, 0)),
                  pl.BlockSpec((None, tt, D_HYENA), lambda i, j: (j, i, 0)),
                  pl.BlockSpec((None, tt, D_HYENA), lambda i, j: (j, i, 0)),
                  pl.BlockSpec((1, D_HYENA), lambda i, j: (0, 0))],
        out_specs=pl.BlockSpec((None, tt, D_HYENA), lambda i, j: (j, i, 0)),
        out_shape=jax.ShapeDtypeStruct((b, n, D_HYENA), BF16),
        compiler_params=_cparams(("arbitrary", "arbitrary")),
        name="inverse_gate",
    )(cmat, sinv, yre, yim, z, gate, bias)


def _hyena_dft_mats(n):
    c, s, alt = _dft_base(n)
    f = jnp.arange(n, dtype=jnp.int32)
    sfwd = jnp.where(f[:, None] == 0, alt[None, :], s).astype(BF16)
    return c.astype(BF16), sfwd, sfwd.T


def _dft_base(n):
    big = 2 * n
    f = jnp.arange(n, dtype=jnp.int32)
    m = (f[:, None] * f[None, :]) % big
    ang = m.astype(F32) * (2.0 * math.pi / big)
    alt = jnp.where(f % 2 == 0, 1.0, -1.0).astype(F32)
    return jnp.cos(ang), -jnp.sin(ang), alt


def _fnet_dft_mats(n):
    c, s, alt = _dft_base(n)
    ce, se = c[0::2], s[0::2]
    c4 = jnp.concatenate([ce, ce * alt[None, :]], 0)
    s4 = jnp.concatenate([se, se * alt[None, :]], 0)
    return c4.astype(BF16), s4.astype(BF16)


def _hyena_features(n):
    t = np.linspace(0.0, 1.0, n, dtype=np.float32)[:, None]
    bands = (HYENA_EMB - 1) // 2
    omega = (2.0 * math.pi / n) * np.arange(n, dtype=np.float32)[:, None]
    fb = np.linspace(1e-4, bands - 1, bands, dtype=np.float32)[None, :]
    z = np.concatenate([t, np.cos(fb * omega), -np.sin(fb * omega)], -1).astype(np.float32)
    return np.pad(z, ((0, 0), (0, LANES - HYENA_EMB)))


def hyena_mixer(u, p, mats):
    b, n, _ = u.shape
    cmat, sfwd, sinv = mats
    xs = short_conv(u, p['short_conv_w'])
    x1, x2, v = xs[0], xs[1], xs[2]
    deltas = np.abs(np.linspace(HYENA_MIN_DECAY, HYENA_MAX_DECAY, D_HYENA, dtype=np.float32))[None, :]
    w1p = jnp.pad(p['filt_w1'], ((0, LANES - HYENA_EMB), (0, 0)))
    h, nyq = hyena_filters(n, jnp.asarray(_hyena_features(n)), w1p, p['filt_b1'][None, :],
                           p['filt_w2'], p['filt_b2'][None, :], p['filt_w3'],
                           p['filt_freq'][None, :], jnp.asarray(deltas))
    kre, kim = filter_spectrum(cmat, sfwd, h)
    kim = kim.at[0].set(nyq.reshape(-1))
    z = v.astype(BF16)
    for o, gate in enumerate((x1, x2)):
        yre, yim = spectral_multiply(cmat, sfwd, z, kre, kim, o)
        z = inverse_gate(cmat, sinv, yre, yim, z, gate, p['hyena_bias'][o][None, :])
    return z


def _fnet_pq_kernel(x_ref, m_ref, p_ref, q_ref):
    pq = jnp.dot(x_ref[...], m_ref[...].astype(BF16), preferred_element_type=F32)
    p_ref[...] = pq[:, :D_FNET].astype(BF16)
    q_ref[...] = pq[:, D_FNET:].astype(BF16)


def _fnet_dft_kernel(c_ref, s_ref, p_ref, q_ref, o_ref):
    o_ref[...] = (jnp.dot(c_ref[...], p_ref[...], preferred_element_type=F32)
                  + jnp.dot(s_ref[...], q_ref[...], preferred_element_type=F32)).astype(o_ref.dtype)


def fourier_mixer(fr, fnet_w, mats):
    b, n, _ = fr.shape
    cmat, smat = mats
    cw = np.arange(FNET_GROUP_W)
    angw = 2.0 * math.pi * ((cw[:, None] * cw[None, :]) % FNET_GROUP_W) / FNET_GROUP_W
    scale = 1.0 / math.sqrt(n * FNET_GROUP_W)
    eye = np.eye(FNET_GROUPS)
    cbd = np.kron(eye, np.cos(angw) * scale).astype(np.float32)
    sbd = np.kron(eye, np.sin(angw) * scale).astype(np.float32)
    wbd = jnp.zeros((D_FNET, D_FNET), F32)
    for g in range(FNET_GROUPS):
        sl = slice(g * FNET_GROUP_W, (g + 1) * FNET_GROUP_W)
        wbd = wbd.at[sl, sl].set(fnet_w[g])
    mcs = small_matmul(jnp.asarray(np.concatenate([cbd, sbd], 0)), wbd)
    m = jnp.concatenate([mcs[:D_FNET], mcs[D_FNET:]], axis=1)
    tm = min(DFT_TILE, n)
    pm, qm = pl.pallas_call(
        _fnet_pq_kernel,
        grid=(b, n // tm),
        in_specs=[pl.BlockSpec((None, tm, D_FNET), lambda i, j: (i, j, 0)),
                  pl.BlockSpec((D_FNET, 2 * D_FNET), lambda i, j: (0, 0))],
        out_specs=[pl.BlockSpec((None, tm, D_FNET), lambda i, j: (i, j, 0))] * 2,
        out_shape=[jax.ShapeDtypeStruct((b, n, D_FNET), BF16)] * 2,
        compiler_params=_cparams(("arbitrary", "arbitrary")),
        name="fnet_channel_dft",
    )(fr, m)
    return pl.pallas_call(
        _fnet_dft_kernel,
        grid=(n // tm, b),
        in_specs=[pl.BlockSpec((tm, n), lambda i, j: (i, 0)),
                  pl.BlockSpec((tm, n), lambda i, j: (i, 0)),
                  pl.BlockSpec((None, n, D_FNET), lambda i, j: (j, 0, 0)),
                  pl.BlockSpec((None, n, D_FNET), lambda i, j: (j, 0, 0))],
        out_specs=pl.BlockSpec((None, tm, D_FNET), lambda i, j: (j, i, 0)),
        out_shape=jax.ShapeDtypeStruct((b, n, D_FNET), BF16),
        compiler_params=_cparams(("arbitrary", "arbitrary")),
        name="fnet_sequence_dft",
    )(cmat, smat, pm, qm)


def _outproj_kernel(sid_ref, hy_ref, at_ref, fn_ref, w_ref, x_ref, mod_ref, g_ref, b_ref, rw_ref,
                    x1_ref, h2_ref, lg_ref):
    del sid_ref
    mix = (jnp.dot(hy_ref[...], w_ref[0:D_HYENA, :], preferred_element_type=F32)
           + jnp.dot(at_ref[...], w_ref[D_HYENA:D_HYENA + D_DIFF, :], preferred_element_type=F32)
           + jnp.dot(fn_ref[...], w_ref[D_HYENA + D_DIFF:, :], preferred_element_type=F32))
    g1 = mod_ref[:, 2 * D_MODEL:3 * D_MODEL]
    sh2 = mod_ref[:, 3 * D_MODEL:4 * D_MODEL]
    sc2 = mod_ref[:, 4 * D_MODEL:5 * D_MODEL]
    x1 = _ln(ALPHA * x_ref[...] + g1 * mix) * g_ref[...] + b_ref[...]
    x1_ref[...] = x1
    h2 = _ln(x1) * (1.0 + sc2) + sh2
    h2_ref[...] = h2
    lg_ref[...] = lax.dot_general(rw_ref[...], h2, (((1,), (1,)), ((), ())),
                                  precision=HIGHEST, preferred_element_type=F32)


def out_projection(hy, at, fn, w_out_bf16, x, mod, ln_g, ln_b, router_wt, sid):
    t = x.shape[0]
    tm = ROW_TILE
    row = lambda w: pl.BlockSpec((tm, w), lambda i, sid: (i, 0))
    grid_spec = pltpu.PrefetchScalarGridSpec(
        num_scalar_prefetch=1,
        grid=(t // tm,),
        in_specs=[row(D_HYENA), row(D_DIFF), row(D_FNET),
                  pl.BlockSpec((D_MIX, D_MODEL), lambda i, sid: (0, 0)),
                  row(D_MODEL),
                  pl.BlockSpec((None, 1, 6 * D_MODEL), lambda i, sid: (sid[i], 0, 0)),
                  pl.BlockSpec((1, D_MODEL), lambda i, sid: (0, 0)),
                  pl.BlockSpec((1, D_MODEL), lambda i, sid: (0, 0)),
                  pl.BlockSpec((N_EXPERTS, D_MODEL), lambda i, sid: (0, 0))],
        out_specs=[row(D_MODEL), row(D_MODEL),
                   pl.BlockSpec((N_EXPERTS, tm), lambda i, sid: (0, i))],
    )
    return pl.pallas_call(
        _outproj_kernel,
        grid_spec=grid_spec,
        out_shape=[jax.ShapeDtypeStruct((t, D_MODEL), F32),
                   jax.ShapeDtypeStruct((t, D_MODEL), F32),
                   jax.ShapeDtypeStruct((N_EXPERTS, t), F32)],
        compiler_params=_cparams(("arbitrary",)),
        name="out_projection",
    )(sid, hy, at, fn, w_out_bf16, x, mod, ln_g, ln_b, router_wt)


def _route_kernel(lg_ref, bias_ref, tri_ref, eidx_ref, gate_ref, rank_ref, cnt_ref, carry_ref):
    @pl.when(pl.program_id(0) == 0)
    def _():
        carry_ref[...] = jnp.zeros_like(carry_ref)

    gsz = N_EXPERTS // N_GROUPS
    neg = -jnp.inf
    s = 1.0 / (1.0 + jnp.exp(-lg_ref[...]))
    sel = s + bias_ref[...]
    tt = s.shape[1]
    sub = lax.broadcasted_iota(jnp.int32, (gsz, tt), 0)
    gs = []
    for g in range(N_GROUPS):
        blk = sel[g * gsz:(g + 1) * gsz, :]
        m1 = blk.max(0, keepdims=True)
        first = jnp.min(jnp.where(blk == m1, sub, gsz), 0, keepdims=True)
        m2 = jnp.where(sub == first, neg, blk).max(0, keepdims=True)
        gs.append(m1 + m2)
    gscore = jnp.concatenate(gs, 0)
    gi = lax.broadcasted_iota(jnp.int32, (N_GROUPS, tt), 0)
    gkeep = jnp.zeros((N_GROUPS, tt), F32)
    cur = gscore
    for _ in range(TOPK_GROUPS):
        mx = cur.max(0, keepdims=True)
        idx = jnp.min(jnp.where(cur == mx, gi, N_GROUPS), 0, keepdims=True)
        pick = gi == idx
        gkeep = jnp.where(pick, 1.0, gkeep)
        cur = jnp.where(pick, neg, cur)
    keep = jnp.concatenate(
        [jnp.broadcast_to(gkeep[g:g + 1, :], (gsz, tt)) for g in range(N_GROUPS)], 0)
    cur = jnp.where(keep > 0.5, sel, neg)
    ei = lax.broadcasted_iota(jnp.int32, (N_EXPERTS, tt), 0)
    picks, idxs, ws = [], [], []
    onehot = jnp.zeros((N_EXPERTS, tt), F32)
    for _ in range(TOP_K):
        mx = cur.max(0, keepdims=True)
        idx = jnp.min(jnp.where(cur == mx, ei, N_EXPERTS), 0, keepdims=True)
        pick = ei == idx
        picks.append(pick)
        idxs.append(idx)
        ws.append(jnp.sum(jnp.where(pick, s, 0.0), 0, keepdims=True))
        cur = jnp.where(pick, neg, cur)
        onehot = jnp.where(pick, 1.0, onehot)
    wsum = ws[0]
    for w in ws[1:]:
        wsum = wsum + w
    inv = ROUTED_SCALE / wsum
    prefix = jnp.dot(onehot.astype(BF16), tri_ref[...], preferred_element_type=F32)
    rank_e = carry_ref[...] + prefix
    eidx_ref[...] = jnp.concatenate(idxs, 0)
    gate_ref[...] = jnp.concatenate([w * inv for w in ws], 0)
    rank_ref[...] = jnp.concatenate(
        [jnp.sum(jnp.where(pk, rank_e, 0.0), 0, keepdims=True) for pk in picks], 0).astype(jnp.int32)
    carry_ref[...] = carry_ref[...] + jnp.sum(onehot, 1, keepdims=True)
    cnt_ref[...] = carry_ref[...]


def route(logits_t, router_b):
    t = logits_t.shape[1]
    tt = ROUTE_TILE
    tri = jnp.asarray(np.triu(np.ones((tt, tt), np.float32), 1), BF16)
    col = lambda: pl.BlockSpec((TOP_K, tt), lambda i: (0, i))
    return pl.pallas_call(
        _route_kernel,
        grid=(t // tt,),
        in_specs=[pl.BlockSpec((N_EXPERTS, tt), lambda i: (0, i)),
                  pl.BlockSpec((N_EXPERTS, 1), lambda i: (0, 0)),
                  pl.BlockSpec((tt, tt), lambda i: (0, 0))],
        out_specs=[col(), col(), col(), pl.BlockSpec((N_EXPERTS, 1), lambda i: (0, 0))],
        out_shape=[jax.ShapeDtypeStruct((TOP_K, t), jnp.int32),
                   jax.ShapeDtypeStruct((TOP_K, t), F32),
                   jax.ShapeDtypeStruct((TOP_K, t), jnp.int32),
                   jax.ShapeDtypeStruct((N_EXPERTS, 1), F32)],
        scratch_shapes=[pltpu.VMEM((N_EXPERTS, 1), F32)],
        compiler_params=_cparams(("arbitrary",)),
        name="route",
    )(logits_t, router_b.reshape(N_EXPERTS, 1), tri)


def _row_copy(src, dst, sem):
    return pltpu.make_async_copy(src, dst, sem)


def _dispatch_kernel(pos_ref, h_ref, xg_in_ref, xg_ref, sem):
    del xg_in_ref
    tm = h_ref.shape[0]

    def issue(t, carry):
        for k in range(TOP_K):
            p = pos_ref[t * TOP_K + k]
            _row_copy(h_ref.at[pl.ds(t, 1), :], xg_ref.at[pl.ds(p, 1), :], sem).start(priority=k % 2)
        return carry

    lax.fori_loop(0, tm, issue, 0)

    def drain(t, carry):
        for k in range(TOP_K):
            _row_copy(h_ref.at[pl.ds(0, 1), :], xg_ref.at[pl.ds(0, 1), :], sem).wait()
        return carry

    lax.fori_loop(0, tm, drain, 0)


def dispatch(pos_flat, h2, xg):
    t = h2.shape[0]
    tm = ROW_TILE
    return pl.pallas_call(
        _dispatch_kernel,
        grid=(t // tm,),
        in_specs=[pl.BlockSpec((tm * TOP_K,), lambda i: (i,), memory_space=pltpu.SMEM),
                  pl.BlockSpec((tm, D_MODEL), lambda i: (i, 0)),
                  pl.BlockSpec(memory_space=pl.ANY)],
        out_specs=pl.BlockSpec(memory_space=pl.ANY),
        out_shape=jax.ShapeDtypeStruct(xg.shape, xg.dtype),
        scratch_shapes=[pltpu.SemaphoreType.DMA(())],
        input_output_aliases={2: 0},
        compiler_params=_cparams(("arbitrary",)),
        name="moe_dispatch",
    )(pos_flat, h2, xg)


def _expert_kernel(be_ref, nu_ref, x_ref, wg_ref, wu_ref, wd_ref, y_ref):
    del be_ref

    @pl.when(pl.program_id(0) < nu_ref[0])
    def _():
        x = x_ref[...].astype(BF16)
        a = (_silu(jnp.dot(x, wg_ref[...], preferred_element_type=F32))
             * jnp.dot(x, wu_ref[...], preferred_element_type=F32))
        y_ref[...] = jnp.dot(a.astype(BF16), wd_ref[...], preferred_element_type=F32)

    @pl.when(pl.program_id(0) >= nu_ref[0])
    def _():
        y_ref[...] = jnp.zeros_like(y_ref)


def expert_ffn(block_exp, n_used, xg, wg, wu, wd):
    p = xg.shape[0]
    blk = MOE_BLK
    grid_spec = pltpu.PrefetchScalarGridSpec(
        num_scalar_prefetch=2,
        grid=(p // blk,),
        in_specs=[pl.BlockSpec((blk, D_MODEL), lambda i, be, nu: (i, 0)),
                  pl.BlockSpec((None, D_MODEL, D_EXPERT), lambda i, be, nu: (be[i], 0, 0)),
                  pl.BlockSpec((None, D_MODEL, D_EXPERT), lambda i, be, nu: (be[i], 0, 0)),
                  pl.BlockSpec((None, D_EXPERT, D_MODEL), lambda i, be, nu: (be[i], 0, 0))],
        out_specs=pl.BlockSpec((blk, D_MODEL), lambda i, be, nu: (i, 0)),
    )
    return pl.pallas_call(
        _expert_kernel,
        grid_spec=grid_spec,
        out_shape=jax.ShapeDtypeStruct((p, D_MODEL), F32),
        compiler_params=_cparams(("arbitrary",)),
        name="expert_ffn",
    )(block_exp, n_used, xg, wg, wu, wd)


def _combine_kernel(sid_ref, pos_ref, yb_ref, gate_ref, h_ref, sg_ref, su_ref, sd_ref, x_ref,
                    mod_ref, g_ref, b_ref, o_ref, buf_ref, sem):
    del sid_ref
    tm = h_ref.shape[0]

    def issue(t, carry):
        for k in range(TOP_K):
            p = pos_ref[t * TOP_K + k]
            _row_copy(yb_ref.at[pl.ds(p, 1), :], buf_ref.at[k, pl.ds(t, 1), :], sem).start(priority=k % 2)
        return carry

    lax.fori_loop(0, tm, issue, 0)
    h = h_ref[...].astype(BF16)
    a = (_silu(jnp.dot(h, sg_ref[...], preferred_element_type=F32))
         * jnp.dot(h, su_ref[...], preferred_element_type=F32))
    y = jnp.dot(a.astype(BF16), sd_ref[...], preferred_element_type=F32)

    def drain(t, carry):
        for k in range(TOP_K):
            _row_copy(yb_ref.at[pl.ds(0, 1), :], buf_ref.at[0, pl.ds(0, 1), :], sem).wait()
        return carry

    lax.fori_loop(0, tm, drain, 0)
    gates = gate_ref[...]
    for k in range(TOP_K):
        y = y + buf_ref[k] * gates[:, k:k + 1]
    g2 = mod_ref[:, 5 * D_MODEL:6 * D_MODEL]
    o_ref[...] = _ln(ALPHA * x_ref[...] + g2 * y) * g_ref[...] + b_ref[...]


def combine(sid, pos_flat, yb, gates_tk, h2, sg, su, sd, x1, mod, ln_g, ln_b):
    t = h2.shape[0]
    tm = COMBINE_TILE
    row = lambda w: pl.BlockSpec((tm, w), lambda i, sid: (i, 0))
    const = lambda shape: pl.BlockSpec(shape, lambda i, sid: (0,) * len(shape))
    grid_spec = pltpu.PrefetchScalarGridSpec(
        num_scalar_prefetch=1,
        grid=(t // tm,),
        in_specs=[pl.BlockSpec((tm * TOP_K,), lambda i, sid: (i,), memory_space=pltpu.SMEM),
                  pl.BlockSpec(memory_space=pl.ANY),
                  row(TOP_K), row(D_MODEL),
                  const((D_MODEL, D_SHARED)), const((D_MODEL, D_SHARED)), const((D_SHARED, D_MODEL)),
                  row(D_MODEL),
                  pl.BlockSpec((None, 1, 6 * D_MODEL), lambda i, sid: (sid[i], 0, 0)),
                  const((1, D_MODEL)), const((1, D_MODEL))],
        out_specs=row(D_MODEL),
        scratch_shapes=[pltpu.VMEM((TOP_K, tm, D_MODEL), F32), pltpu.SemaphoreType.DMA(())],
    )
    return pl.pallas_call(
        _combine_kernel,
        grid_spec=grid_spec,
        out_shape=jax.ShapeDtypeStruct((t, D_MODEL), F32),
        compiler_params=_cparams(("arbitrary",)),
        name="moe_combine",
    )(sid, pos_flat, yb, gates_tk, h2, sg, su, sd, x1, mod, ln_g, ln_b)


def _moe_padded_rows(t):
    a = t * TOP_K
    n_blocks = (a + N_EXPERTS * (MOE_BLK - 1) + MOE_BLK - 1) // MOE_BLK
    return n_blocks, n_blocks * MOE_BLK


def moe_slots(eidx, rank, counts, t):
    n_blocks, _ = _moe_padded_rows(t)
    cnt = counts.reshape(-1).astype(jnp.int32)
    padded = (cnt + MOE_BLK - 1) // MOE_BLK * MOE_BLK
    cum_pad = jnp.cumsum(padded)
    pstart = cum_pad - padded
    experts = jnp.arange(N_EXPERTS, dtype=jnp.int32)
    pos = jnp.sum(jnp.where(eidx[None] == experts[:, None, None], pstart[:, None, None], 0), 0) + rank
    starts = jnp.arange(n_blocks, dtype=jnp.int32) * MOE_BLK
    block_exp = jnp.minimum(jnp.sum((cum_pad[None, :] <= starts[:, None]).astype(jnp.int32), 1),
                            N_EXPERTS - 1)
    n_used = (cum_pad[-1] // MOE_BLK).astype(jnp.int32).reshape(1)
    return pos.T.reshape(-1), block_exp, n_used


def _rope_tables():
    n = DIFF_QK // 4
    inv = 1.0 / (ROPE_BASE ** (np.arange(n, dtype=np.float32) / n))
    t = np.arange(SEQ)
    row = (t // GRID_W).astype(np.float32)
    col = (t % GRID_W).astype(np.float32)
    ang = np.stack([row[:, None] * inv, col[:, None] * inv], axis=1).astype(np.float32)
    cos = np.repeat(np.cos(ang).reshape(SEQ, 2 * n), 2, axis=1)
    sin = np.repeat(np.sin(ang).reshape(SEQ, 2 * n), 2, axis=1)
    sign = np.tile(np.array([-1.0, 1.0], np.float32), DIFF_QK // 2)
    sin = sin * sign[None, :]
    cos = np.tile(cos, (1, LANES // DIFF_QK))
    sin = np.tile(sin, (1, LANES // DIFF_QK))
    cos = np.concatenate([cos, np.ones((ROW_TILE, LANES), np.float32)], 0)
    sin = np.concatenate([sin, np.zeros((ROW_TILE, LANES), np.float32)], 0)
    return jnp.asarray(cos, F32), jnp.asarray(sin, F32)


def _tile_ids(t, tile):
    n = t // tile
    i = np.arange(n)
    lat = T_LAT // tile
    sid = np.where(i < lat, i // (SEQ // tile), BATCH).astype(np.int32)
    rid = np.where(i < lat, i % (SEQ // tile), SEQ // tile).astype(np.int32)
    return jnp.asarray(sid), jnp.asarray(rid)


def kernel(x, c, ctx, c_ctx, w_mod, b_mod, w_in, w_out, short_conv_w, filt_w1, filt_b1, filt_w2,
           filt_b2, filt_w3, filt_freq, hyena_bias, lam_q1, lam_k1, lam_q2, lam_k2, subln_w, fnet_w,
           ln1_g, ln1_b, ln2_g, ln2_b, router_w, router_b, exp_w_gate, exp_w_up, exp_w_down,
           sh_w_gate, sh_w_up, sh_w_down):
    xt = jnp.concatenate([x.reshape(T_LAT, D_MODEL), ctx.reshape(T_CTX, D_MODEL)], 0)
    cc = jnp.concatenate([c, c_ctx[None, :], jnp.zeros((3, D_MODEL), F32)], 0)
    cos_t, sin_t = _rope_tables()
    sid_all, rid_all = _tile_ids(T_ALL, ROW_TILE)
    hy_mats = {SEQ: _hyena_dft_mats(SEQ), CTX_LEN: _hyena_dft_mats(CTX_LEN)}
    fn_mats = {SEQ: _fnet_dft_mats(SEQ), CTX_LEN: _fnet_dft_mats(CTX_LEN)}
    _, p_rows = _moe_padded_rows(T_ALL)
    xg = jnp.zeros((p_rows, D_MODEL), F32)

    for li in range(DEPTH):
        last = li == DEPTH - 1
        lam_init = 0.8 - 0.6 * math.exp(-0.3 * li)
        p = {'short_conv_w': short_conv_w[li], 'filt_w1': filt_w1[li], 'filt_b1': filt_b1[li],
             'filt_w2': filt_w2[li], 'filt_b2': filt_b2[li], 'filt_w3': filt_w3[li],
             'filt_freq': filt_freq[li], 'hyena_bias': hyena_bias[li]}
        mod = small_matmul(cc, w_mod[li], b_mod[li][None, :], silu_in=True, tn=6 * D_MODEL // 4)
        mod = mod.reshape(8, 1, 6 * D_MODEL)
        lam = (jnp.exp(jnp.sum(lam_q1[li] * lam_k1[li])) - jnp.exp(jnp.sum(lam_q2[li] * lam_k2[li]))
               + lam_init).reshape(1).astype(F32)

        hy, q, k, v, fr = in_projection(xt, mod, w_in[li].astype(BF16), cos_t, sin_t, sid_all, rid_all)
        subln2 = jnp.tile(subln_w[li], 2)[None, :]
        t_act = T_LAT if last else T_ALL
        att = diff_attention(lam, q, k, v, subln2, lam_init=lam_init, latent=True)
        hyo = hyena_mixer(hy[:T_LAT].reshape(BATCH, SEQ, HY_W), p, hy_mats[SEQ]).reshape(T_LAT, D_HYENA)
        fno = fourier_mixer(fr[:T_LAT].reshape(BATCH, SEQ, D_FNET), fnet_w[li],
                            fn_mats[SEQ]).reshape(T_LAT, D_FNET)
        if not last:
            att_c = diff_attention(lam, q, k, v, subln2, lam_init=lam_init, latent=False)
            hyo_c = hyena_mixer(hy[T_LAT:].reshape(BATCH, CTX_LEN, HY_W), p, hy_mats[CTX_LEN])
            fno_c = fourier_mixer(fr[T_LAT:].reshape(BATCH, CTX_LEN, D_FNET), fnet_w[li],
                                  fn_mats[CTX_LEN])
            att = jnp.concatenate([att, att_c], 0)
            hyo = jnp.concatenate([hyo, hyo_c.reshape(T_CTX, D_HYENA)], 0)
            fno = jnp.concatenate([fno, fno_c.reshape(T_CTX, D_FNET)], 0)

        sid_r, _ = _tile_ids(t_act, ROW_TILE)
        x1, h2, logits_t = out_projection(
            hyo, att, fno, w_out[li].astype(BF16), xt[:t_act], mod,
            ln1_g[li][None, :], ln1_b[li][None, :], router_w[li].T, sid_r)

        eidx, gates, rank, counts = route(logits_t, router_b[li])
        pos_flat, block_exp, n_used = moe_slots(eidx, rank, counts, T_ALL)
        xg = dispatch(pos_flat, h2, xg)
        yb = expert_ffn(block_exp, n_used, xg, exp_w_gate[li].astype(BF16),
                        exp_w_up[li].astype(BF16), exp_w_down[li].astype(BF16))
        sid_c, _ = _tile_ids(t_act, COMBINE_TILE)
        xt = combine(sid_c, pos_flat, yb, gates.T, h2, sh_w_gate[li].astype(BF16),
                     sh_w_up[li].astype(BF16), sh_w_down[li].astype(BF16), x1, mod,
                     ln2_g[li][None, :], ln2_b[li][None, :])
    return xt[:T_LAT].reshape(BATCH, SEQ, D_MODEL)
```

```python
import functools
import math

import jax
import jax.numpy as jnp
import numpy as np
from jax import lax
from jax.experimental import pallas as pl
from jax.experimental.pallas import tpu as pltpu

F32 = jnp.float32
BF16 = jnp.bfloat16
HIGHEST = lax.Precision.HIGHEST

D_MODEL = 1024
BATCH = 4
SEQ = 4096
DEPTH = 4
GRID_W = 64
CTX_LEN = 256
D_HYENA = 256
HYENA_ORDER = 2
HYENA_EMB = 33
HYENA_FILTER_WIDTH = 64
HYENA_TARGET = 1e-2
HYENA_MIN_DECAY = math.log(HYENA_TARGET) / 1.5
HYENA_MAX_DECAY = math.log(HYENA_TARGET) / 0.3
SHORT_CONV = 3
N_DIFF_HEADS = 8
DIFF_QK = 32
DIFF_V = 64
D_DIFF = N_DIFF_HEADS * DIFF_V
D_FNET = 256
FNET_GROUPS = 4
FNET_GROUP_W = D_FNET // FNET_GROUPS
D_MIX = D_HYENA + D_DIFF + D_FNET
ROPE_BASE = 10000.0
HY_W = 3 * D_HYENA
QK_W = N_DIFF_HEADS * 2 * DIFF_QK
Q_OFF = HY_W
K_OFF = Q_OFF + QK_W
V_OFF = K_OFF + QK_W
F_OFF = V_OFF + D_DIFF
D_IN = F_OFF + D_FNET
N_EXPERTS = 64
TOP_K = 8
N_GROUPS = 8
TOPK_GROUPS = 4
D_EXPERT = 256
D_SHARED = 256
ROUTED_SCALE = 2.5
ALPHA = (2 * DEPTH) ** 0.25
LN_EPS = 1e-5

T_LAT = BATCH * SEQ
T_CTX = BATCH * CTX_LEN
T_ALL = T_LAT + T_CTX

LANES = 128
VMEM_LIMIT = 56 * 1024 * 1024
ROW_TILE = 256
ATTN_TQ = 256
DFT_TILE = 512
ROUTE_TILE = 512
MOE_BLK = 256
COMBINE_TILE = 128
LOG2E = 1.4426950408889634


def _cparams(sem):
    return pltpu.CompilerParams(dimension_semantics=sem, vmem_limit_bytes=VMEM_LIMIT)


def _ln(x):
    mu = jnp.mean(x, -1, keepdims=True)
    xc = x - mu
    var = jnp.mean(xc * xc, -1, keepdims=True)
    return xc * lax.rsqrt(var + LN_EPS)


def _silu(x):
    return x * (1.0 / (1.0 + jnp.exp(-x)))


def _pack_bf16_pairs(x):
    w = x.shape[1] // 2
    lo = pltpu.bitcast(x[:, :w].astype(BF16).astype(F32), jnp.uint32) >> 16
    hi = pltpu.bitcast(x[:, w:].astype(BF16).astype(F32), jnp.uint32) & jnp.uint32(0xFFFF0000)
    return lo | hi


def _unpack_bf16_pairs(p):
    lo = pltpu.bitcast(p << 16, F32)
    hi = pltpu.bitcast(p & jnp.uint32(0xFFFF0000), F32)
    return jnp.concatenate([lo, hi], axis=1)


def _small_mm_kernel(a_ref, w_ref, b_ref, o_ref, *, silu_in):
    a = a_ref[...]
    if silu_in:
        a = _silu(a)
    o_ref[...] = jnp.dot(a, w_ref[...], precision=HIGHEST, preferred_element_type=F32) + b_ref[...]


def small_matmul(a, w, b=None, *, silu_in=False, tn=None):
    m, k = a.shape
    n = w.shape[1]
    tn = n if tn is None else tn
    if b is None:
        b = jnp.zeros((1, n), F32)
    return pl.pallas_call(
        functools.partial(_small_mm_kernel, silu_in=silu_in),
        grid=(n // tn,),
        in_specs=[pl.BlockSpec((m, k), lambda j: (0, 0)),
                  pl.BlockSpec((k, tn), lambda j: (0, j)),
                  pl.BlockSpec((1, tn), lambda j: (0, j))],
        out_specs=pl.BlockSpec((m, tn), lambda j: (0, j)),
        out_shape=jax.ShapeDtypeStruct((m, n), F32),
        compiler_params=_cparams(("arbitrary",)),
        name="small_matmul",
    )(a, w, b)


def _inproj_kernel(sid_ref, rid_ref, x_ref, mod_ref, w_ref, cos_ref, sin_ref,
                   hy_ref, q_ref, k_ref, v_ref, fr_ref):
    del sid_ref, rid_ref
    x = x_ref[...]
    sh = mod_ref[:, 0:D_MODEL]
    sc = mod_ref[:, D_MODEL:2 * D_MODEL]
    h = (_ln(x) * (1.0 + sc) + sh).astype(BF16)
    z = jnp.dot(h, w_ref[...], preferred_element_type=F32)
    hy_ref[...] = z[:, :HY_W]
    cos = jnp.concatenate([cos_ref[...]] * (QK_W // LANES), axis=1)
    sin = jnp.concatenate([sin_ref[...]] * (QK_W // LANES), axis=1)
    lane = lax.broadcasted_iota(jnp.int32, (1, QK_W), 1)
    even = (lane % 2) == 0

    def rope(t):
        nxt = pltpu.roll(t, QK_W - 1, 1)
        prv = pltpu.roll(t, 1, 1)
        return t * cos + jnp.where(even, nxt, prv) * sin

    qscale = (DIFF_QK ** -0.5) * LOG2E
    q_ref[...] = (rope(z[:, Q_OFF:K_OFF]) * qscale).astype(BF16)
    k_ref[...] = rope(z[:, K_OFF:V_OFF]).astype(BF16)
    v_ref[...] = z[:, V_OFF:F_OFF].astype(BF16)
    fr_ref[...] = z[:, F_OFF:].astype(BF16)


def in_projection(x, mod, w_in_bf16, cos_t, sin_t, sid, rid):
    t = x.shape[0]
    tm = ROW_TILE
    grid_spec = pltpu.PrefetchScalarGridSpec(
        num_scalar_prefetch=2,
        grid=(t // tm,),
        in_specs=[
            pl.BlockSpec((tm, D_MODEL), lambda i, sid, rid: (i, 0)),
            pl.BlockSpec((None, 1, 6 * D_MODEL), lambda i, sid, rid: (sid[i], 0, 0)),
            pl.BlockSpec((D_MODEL, D_IN), lambda i, sid, rid: (0, 0)),
            pl.BlockSpec((tm, LANES), lambda i, sid, rid: (rid[i], 0)),
            pl.BlockSpec((tm, LANES), lambda i, sid, rid: (rid[i], 0)),
        ],
        out_specs=[
            pl.BlockSpec((tm, HY_W), lambda i, sid, rid: (i, 0)),
            pl.BlockSpec((tm, QK_W), lambda i, sid, rid: (i, 0)),
            pl.BlockSpec((tm, QK_W), lambda i, sid, rid: (i, 0)),
            pl.BlockSpec((tm, D_DIFF), lambda i, sid, rid: (i, 0)),
            pl.BlockSpec((tm, D_FNET), lambda i, sid, rid: (i, 0)),
        ],
    )
    return pl.pallas_call(
        _inproj_kernel,
        grid_spec=grid_spec,
        out_shape=[jax.ShapeDtypeStruct((t, HY_W), F32),
                   jax.ShapeDtypeStruct((t, QK_W), BF16),
                   jax.ShapeDtypeStruct((t, QK_W), BF16),
                   jax.ShapeDtypeStruct((t, D_DIFF), BF16),
                   jax.ShapeDtypeStruct((t, D_FNET), BF16)],
        compiler_params=_cparams(("arbitrary",)),
        name="in_projection",
    )(sid, rid, x, mod, w_in_bf16, cos_t, sin_t)


def _attn_kernel(lam_ref, q_ref, *refs, lam_init, n_seg):
    k_refs = refs[:n_seg]
    v_refs = refs[n_seg:2 * n_seg]
    w_ref = refs[2 * n_seg]
    o_ref = refs[2 * n_seg + 1]
    q = q_ref[...]
    lane = lax.broadcasted_iota(jnp.int32, (1, LANES), 1)
    lam = lam_ref[0]
    nt = (((1,), (1,)), ((), ()))
    first = lane < DIFF_V
    scores = []
    for u in range(4):
        lo = (u // 2) * DIFF_V + (u % 2) * DIFF_QK
        qm = jnp.where((lane >= lo) & (lane < lo + DIFF_QK), q, jnp.zeros_like(q))
        scores.append([lax.dot_general(qm, kr[...], nt, preferred_element_type=F32) for kr in k_refs])
    probs = []
    for s in scores:
        mx = s[0].max(-1, keepdims=True)
        for sp in s[1:]:
            mx = jnp.maximum(mx, sp.max(-1, keepdims=True))
        e = [jnp.exp2(sp - mx) for sp in s]
        den = e[0].sum(-1, keepdims=True)
        for ep in e[1:]:
            den = den + ep.sum(-1, keepdims=True)
        probs.append((e, 1.0 / den))
    heads = []
    for h in range(2):
        (e0, r0), (e1, r1) = probs[2 * h], probs[2 * h + 1]
        r1 = lam * r1
        o = None
        for seg in range(n_seg):
            w = (e0[seg] * r0 - e1[seg] * r1).astype(BF16)
            part = jnp.dot(w, v_refs[seg][...], preferred_element_type=F32)
            o = part if o is None else o + part
        heads.append(o)
    o = jnp.where(first, heads[0], heads[1])
    sq = o * o
    ms0 = jnp.sum(jnp.where(first, sq, 0.0), -1, keepdims=True)
    ms1 = jnp.sum(jnp.where(first, 0.0, sq), -1, keepdims=True)
    ms = jnp.where(first, ms0, ms1) * (1.0 / DIFF_V)
    y = o * lax.rsqrt(ms + LN_EPS) * w_ref[...] * (1.0 - lam_init)
    o_ref[...] = y.astype(o_ref.dtype)


def diff_attention(lam, q, k, v, subln2, *, lam_init, latent):
    tq = ATTN_TQ
    n_q = (SEQ if latent else CTX_LEN) // tq
    q_base = 0 if latent else T_LAT // tq
    ctx_blk = T_LAT // CTX_LEN

    specs, args = [], []
    for arr in (k, v):
        if latent:
            specs.append(pl.BlockSpec((SEQ, LANES), lambda b, hp, i: (b, hp)))
            args.append(arr)
        specs.append(pl.BlockSpec((CTX_LEN, LANES), lambda b, hp, i: (ctx_blk + b, hp)))
        args.append(arr)
    n_seg = 2 if latent else 1
    rows = T_LAT if latent else T_CTX
    return pl.pallas_call(
        functools.partial(_attn_kernel, lam_init=lam_init, n_seg=n_seg),
        grid=(BATCH, N_DIFF_HEADS // 2, n_q),
        in_specs=[pl.BlockSpec(memory_space=pltpu.SMEM),
                  pl.BlockSpec((tq, LANES), lambda b, hp, i: (q_base + b * n_q + i, hp))] + specs
                 + [pl.BlockSpec((1, LANES), lambda b, hp, i: (0, 0))],
        out_specs=pl.BlockSpec((tq, LANES), lambda b, hp, i: (b * n_q + i, hp)),
        out_shape=jax.ShapeDtypeStruct((rows, D_DIFF), BF16),
        compiler_params=_cparams(("arbitrary", "arbitrary", "arbitrary")),
        name="diff_attention_lat" if latent else "diff_attention_ctx",
    )(lam, q, *args, subln2)


def _short_conv_kernel(u_ref, w_ref, o_ref):
    u = u_ref[...]
    n = u.shape[0]
    row = lax.broadcasted_iota(jnp.int32, (n, 1), 0)
    prv = jnp.where(row == 0, 0.0, pltpu.roll(u, 1, 0))
    nxt = jnp.where(row == n - 1, 0.0, pltpu.roll(u, n - 1, 0))
    o_ref[...] = prv * w_ref[0:1, :] + u * w_ref[1:2, :] + nxt * w_ref[2:3, :]


def short_conv(u, w):
    b, n, _ = u.shape
    return pl.pallas_call(
        _short_conv_kernel,
        grid=(b, 3),
        in_specs=[pl.BlockSpec((None, n, D_HYENA), lambda i, j: (i, 0, j)),
                  pl.BlockSpec((SHORT_CONV, D_HYENA), lambda i, j: (0, j))],
        out_specs=pl.BlockSpec((None, None, n, D_HYENA), lambda i, j: (j, i, 0, 0)),
        out_shape=jax.ShapeDtypeStruct((3, b, n, D_HYENA), F32),
        compiler_params=_cparams(("arbitrary", "arbitrary")),
        name="short_conv",
    )(u, w)


def _filter_kernel(z_ref, w1_ref, b1_ref, w2_ref, b2_ref, w3_ref, fq_ref, dl_ref, h_ref, ny_ref):
    n = z_ref.shape[0]
    fq = fq_ref[...]
    a = jnp.sin(fq * (jnp.dot(z_ref[...], w1_ref[...], precision=HIGHEST,
                              preferred_element_type=F32) + b1_ref[...]))
    a = jnp.sin(fq * (jnp.dot(a, w2_ref[...], precision=HIGHEST,
                              preferred_element_type=F32) + b2_ref[...]))
    row = lax.broadcasted_iota(jnp.int32, (n, 1), 0)
    t = row.astype(F32) * (1.0 / (n - 1))
    win = jnp.exp(-t * dl_ref[...])
    alt = jnp.where(row % 2 == 0, 1.0, -1.0)
    hf = jnp.dot(a, w3_ref[:, 0:D_HYENA], precision=HIGHEST, preferred_element_type=F32) * win
    hb = jnp.dot(a, w3_ref[:, D_HYENA:], precision=HIGHEST, preferred_element_type=F32) * win
    hb0 = jnp.where(row == 0, 0.0, hb)
    l1 = jnp.sum(jnp.abs(hf), 0, keepdims=True) + jnp.sum(jnp.abs(hb0), 0, keepdims=True)
    inv = 1.0 / l1
    hf = hf * inv
    hb0 = hb0 * inv
    ks = hf + hb0
    h_ref[:, 0:D_HYENA] = ks
    h_ref[:, D_HYENA:] = hf - hb0
    ny_ref[...] = jnp.sum(ks * alt, 0, keepdims=True)


def hyena_filters(n, zfeat, w1p, b1, w2, b2, w3, freq, deltas):
    full = lambda shape: pl.BlockSpec(shape, lambda o: (0,) * len(shape))
    return pl.pallas_call(
        _filter_kernel,
        grid=(HYENA_ORDER,),
        in_specs=[full(zfeat.shape), full(w1p.shape), full(b1.shape), full(w2.shape),
                  full(b2.shape),
                  pl.BlockSpec((HYENA_FILTER_WIDTH, 2 * D_HYENA), lambda o: (0, o)),
                  full(freq.shape), full(deltas.shape)],
        out_specs=[pl.BlockSpec((n, 2 * D_HYENA), lambda o: (0, o)),
                   pl.BlockSpec((None, 1, D_HYENA), lambda o: (o, 0, 0))],
        out_shape=[jax.ShapeDtypeStruct((n, 4 * D_HYENA), F32),
                   jax.ShapeDtypeStruct((HYENA_ORDER, 1, D_HYENA), F32)],
        compiler_params=_cparams(("arbitrary",)),
        name="hyena_filters",
    )(zfeat, w1p, b1, w2, b2, w3, freq, deltas)


def _dft_fwd_kernel(c_ref, s_ref, a_ref, b_ref, re_ref, im_ref):
    re_ref[...] = jnp.dot(c_ref[...], a_ref[...].astype(BF16), preferred_element_type=F32)
    im_ref[...] = jnp.dot(s_ref[...], b_ref[...].astype(BF16), preferred_element_type=F32)


def filter_spectrum(cmat, smat, h):
    n = cmat.shape[0]
    tf = min(DFT_TILE, n)
    return pl.pallas_call(
        _dft_fwd_kernel,
        grid=(n // tf, HYENA_ORDER),
        in_specs=[pl.BlockSpec((tf, n), lambda i, o: (i, 0)),
                  pl.BlockSpec((tf, n), lambda i, o: (i, 0)),
                  pl.BlockSpec((n, D_HYENA), lambda i, o: (0, 2 * o)),
                  pl.BlockSpec((n, D_HYENA), lambda i, o: (0, 2 * o + 1))],
        out_specs=[pl.BlockSpec((tf, D_HYENA), lambda i, o: (i, o)),
                   pl.BlockSpec((tf, D_HYENA), lambda i, o: (i, o))],
        out_shape=[jax.ShapeDtypeStruct((n, HYENA_ORDER * D_HYENA), F32)] * 2,
        compiler_params=_cparams(("arbitrary", "arbitrary")),
        name="filter_spectrum",
    )(cmat, smat, h, h)


def _spec_mul_kernel(c_ref, s_ref, z_ref, kre_ref, kim_ref, yre_ref, yim_ref, *, n):
    z = z_ref[...]
    ure = jnp.dot(c_ref[...], z, preferred_element_type=F32)
    uim = jnp.dot(s_ref[...], z, preferred_element_type=F32)
    kre = kre_ref[...]
    kim = kim_ref[...]
    tf = ure.shape[0]
    grow = pl.program_id(0) * tf + lax.broadcasted_iota(jnp.int32, (tf, 1), 0)
    dc = grow == 0
    yre = jnp.where(dc, ure * kre, ure * kre - uim * kim)
    yim = jnp.where(dc, uim * kim, ure * kim + uim * kre)
    wgt = jnp.where(dc, 1.0 / (2 * n), 2.0 / (2 * n))
    yre_ref[...] = (yre * wgt).astype(BF16)
    yim_ref[...] = (yim * wgt).astype(BF16)


def spectral_multiply(cmat, sfwd, z, kre, kim, order):
    b, n, _ = z.shape
    tf = min(DFT_TILE, n)
    return pl.pallas_call(
        functools.partial(_spec_mul_kernel, n=n),
        grid=(n // tf, b),
        in_specs=[pl.BlockSpec((tf, n), lambda i, j: (i, 0)),
                  pl.BlockSpec((tf, n), lambda i, j: (i, 0)),
                  pl.BlockSpec((None, n, D_HYENA), lambda i, j: (j, 0, 0)),
                  pl.BlockSpec((tf, D_HYENA), lambda i, j: (i, order)),
                  pl.BlockSpec((tf, D_HYENA), lambda i, j: (i, order))],
        out_specs=[pl.BlockSpec((None, tf, D_HYENA), lambda i, j: (j, i, 0)),
                   pl.BlockSpec((None, tf, D_HYENA), lambda i, j: (j, i, 0))],
        out_shape=[jax.ShapeDtypeStruct((b, n, D_HYENA), BF16)] * 2,
        compiler_params=_cparams(("arbitrary", "arbitrary")),
        name="spectral_multiply",
    )(cmat, sfwd, z, kre, kim)


def _inv_gate_kernel(c_ref, s_ref, yre_ref, yim_ref, z_ref, g_ref, bias_ref, o_ref):
    y = (jnp.dot(c_ref[...], yre_ref[...], preferred_element_type=F32)
         + jnp.dot(s_ref[...], yim_ref[...], preferred_element_type=F32))
    z = z_ref[...].astype(F32)
    o_ref[...] = (g_ref[...] * (y + z * bias_ref[...])).astype(o_ref.dtype)


def inverse_gate(cmat, sinv, yre, yim, z, gate, bias):
    b, n, _ = z.shape
    tt = min(DFT_TILE, n)
    return pl.pallas_call(
        _inv_gate_kernel,
        grid=(n // tt, b),
        in_specs=[pl.BlockSpec((tt, n), lambda i, j: (i, 0)),
                  pl.BlockSpec((tt, n), lambda i, j: (i, 0)),
                  pl.BlockSpec((None, n, D_HYENA), lambda i, j: (j, 0, 0)),
                  pl.BlockSpec((None, n, D_HYENA), lambda i, j: (j, 0, 0)),
                  pl.BlockSpec((None, tt, D_HYENA), lambda i, j: (j, i, 0)),
                  pl.BlockSpec((None, tt, D_HYENA), lambda i, j: (j, i, 0)),
                  pl.BlockSpec((1, D_HYENA), lambda i, j: (0, 0))],
        out_specs=pl.BlockSpec((None, tt, D_HYENA), lambda i, j: (j, i, 0)),
        out_shape=jax.ShapeDtypeStruct((b, n, D_HYENA), BF16),
        compiler_params=_cparams(("arbitrary", "arbitrary")),
        name="inverse_gate",
    )(cmat, sinv, yre, yim, z, gate, bias)


def _hyena_dft_mats(n, base=None):
    c, s, alt = _dft_base(n) if base is None else base
    f = jnp.arange(n, dtype=jnp.int32)
    sfwd = jnp.where(f[:, None] == 0, alt[None, :], s).astype(BF16)
    return c.astype(BF16), sfwd, sfwd.T


def _dft_base(n):
    big = 2 * n
    f = jnp.arange(n, dtype=jnp.int32)
    m = (f[:, None] * f[None, :]) % big
    ang = m.astype(F32) * (2.0 * math.pi / big)
    alt = jnp.where(f % 2 == 0, 1.0, -1.0).astype(F32)
    c, s = lax.optimization_barrier((jnp.cos(ang), -jnp.sin(ang)))
    return c, s, alt


def _fnet_dft_mats(n, base=None):
    c, s, alt = _dft_base(n) if base is None else base
    ce, se = c[0::2], s[0::2]
    c4 = jnp.concatenate([ce, ce * alt[None, :]], 0)
    s4 = jnp.concatenate([se, se * alt[None, :]], 0)
    return c4.astype(BF16), s4.astype(BF16)


def _hyena_features(n):
    t = np.linspace(0.0, 1.0, n, dtype=np.float32)[:, None]
    bands = (HYENA_EMB - 1) // 2
    omega = (2.0 * math.pi / n) * np.arange(n, dtype=np.float32)[:, None]
    fb = np.linspace(1e-4, bands - 1, bands, dtype=np.float32)[None, :]
    z = np.concatenate([t, np.cos(fb * omega), -np.sin(fb * omega)], -1).astype(np.float32)
    return np.pad(z, ((0, 0), (0, LANES - HYENA_EMB)))


def hyena_mixer(u, p, mats):
    b, n, _ = u.shape
    cmat, sfwd, sinv = mats
    xs = short_conv(u, p['short_conv_w'])
    x1, x2, v = xs[0], xs[1], xs[2]
    deltas = np.abs(np.linspace(HYENA_MIN_DECAY, HYENA_MAX_DECAY, D_HYENA, dtype=np.float32))[None, :]
    w1p = jnp.pad(p['filt_w1'], ((0, LANES - HYENA_EMB), (0, 0)))
    h, nyq = hyena_filters(n, jnp.asarray(_hyena_features(n)), w1p, p['filt_b1'][None, :],
                           p['filt_w2'], p['filt_b2'][None, :], p['filt_w3'],
                           p['filt_freq'][None, :], jnp.asarray(deltas))
    kre, kim = filter_spectrum(cmat, sfwd, h)
    kim = kim.at[0].set(nyq.reshape(-1))
    z = v.astype(BF16)
    for o, gate in enumerate((x1, x2)):
        yre, yim = spectral_multiply(cmat, sfwd, z, kre, kim, o)
        z = inverse_gate(cmat, sinv, yre, yim, z, gate, p['hyena_bias'][o][None, :])
    return z


def _fnet_pq_kernel(x_ref, m_ref, p_ref, q_ref):
    pq = jnp.dot(x_ref[...], m_ref[...].astype(BF16), preferred_element_type=F32)
    p_ref[...] = pq[:, :D_FNET].astype(BF16)
    q_ref[...] = pq[:, D_FNET:].astype(BF16)


def _fnet_dft_kernel(c_ref, s_ref, p_ref, q_ref, o_ref):
    o_ref[...] = (jnp.dot(c_ref[...], p_ref[...], preferred_element_type=F32)
                  + jnp.dot(s_ref[...], q_ref[...], preferred_element_type=F32)).astype(o_ref.dtype)


def fourier_mixer(fr, fnet_w, mats):
    b, n, _ = fr.shape
    cmat, smat = mats
    cw = np.arange(FNET_GROUP_W)
    angw = 2.0 * math.pi * ((cw[:, None] * cw[None, :]) % FNET_GROUP_W) / FNET_GROUP_W
    scale = 1.0 / math.sqrt(n * FNET_GROUP_W)
    eye = np.eye(FNET_GROUPS)
    cbd = np.kron(eye, np.cos(angw) * scale).astype(np.float32)
    sbd = np.kron(eye, np.sin(angw) * scale).astype(np.float32)
    wbd = jnp.zeros((D_FNET, D_FNET), F32)
    for g in range(FNET_GROUPS):
        sl = slice(g * FNET_GROUP_W, (g + 1) * FNET_GROUP_W)
        wbd = wbd.at[sl, sl].set(fnet_w[g])
    mcs = small_matmul(jnp.asarray(np.concatenate([cbd, sbd], 0)), wbd)
    m = jnp.concatenate([mcs[:D_FNET], mcs[D_FNET:]], axis=1)
    tm = min(DFT_TILE, n)
    pm, qm = pl.pallas_call(
        _fnet_pq_kernel,
        grid=(b, n // tm),
        in_specs=[pl.BlockSpec((None, tm, D_FNET), lambda i, j: (i, j, 0)),
                  pl.BlockSpec((D_FNET, 2 * D_FNET), lambda i, j: (0, 0))],
        out_specs=[pl.BlockSpec((None, tm, D_FNET), lambda i, j: (i, j, 0))] * 2,
        out_shape=[jax.ShapeDtypeStruct((b, n, D_FNET), BF16)] * 2,
        compiler_params=_cparams(("arbitrary", "arbitrary")),
        name="fnet_channel_dft",
    )(fr, m)
    return pl.pallas_call(
        _fnet_dft_kernel,
        grid=(n // tm, b),
        in_specs=[pl.BlockSpec((tm, n), lambda i, j: (i, 0)),
                  pl.BlockSpec((tm, n), lambda i, j: (i, 0)),
                  pl.BlockSpec((None, n, D_FNET), lambda i, j: (j, 0, 0)),
                  pl.BlockSpec((None, n, D_FNET), lambda i, j: (j, 0, 0))],
        out_specs=pl.BlockSpec((None, tm, D_FNET), lambda i, j: (j, i, 0)),
        out_shape=jax.ShapeDtypeStruct((b, n, D_FNET), BF16),
        compiler_params=_cparams(("arbitrary", "arbitrary")),
        name="fnet_sequence_dft",
    )(cmat, smat, pm, qm)


def _outproj_kernel(sid_ref, hy_ref, at_ref, fn_ref, w_ref, x_ref, mod_ref, g_ref, b_ref, rw_ref,
                    x1_ref, h2_ref, lg_ref):
    del sid_ref
    mix = (jnp.dot(hy_ref[...], w_ref[0:D_HYENA, :], preferred_element_type=F32)
           + jnp.dot(at_ref[...], w_ref[D_HYENA:D_HYENA + D_DIFF, :], preferred_element_type=F32)
           + jnp.dot(fn_ref[...], w_ref[D_HYENA + D_DIFF:, :], preferred_element_type=F32))
    g1 = mod_ref[:, 2 * D_MODEL:3 * D_MODEL]
    sh2 = mod_ref[:, 3 * D_MODEL:4 * D_MODEL]
    sc2 = mod_ref[:, 4 * D_MODEL:5 * D_MODEL]
    x1 = _ln(ALPHA * x_ref[...] + g1 * mix) * g_ref[...] + b_ref[...]
    x1_ref[...] = x1
    h2 = _ln(x1) * (1.0 + sc2) + sh2
    h2_ref[...] = _pack_bf16_pairs(h2)
    lg_ref[...] = lax.dot_general(rw_ref[...], h2, (((1,), (1,)), ((), ())),
                                  precision=HIGHEST, preferred_element_type=F32)


def out_projection(hy, at, fn, w_out_bf16, x, mod, ln_g, ln_b, router_wt, sid):
    t = x.shape[0]
    tm = ROW_TILE
    row = lambda w: pl.BlockSpec((tm, w), lambda i, sid: (i, 0))
    grid_spec = pltpu.PrefetchScalarGridSpec(
        num_scalar_prefetch=1,
        grid=(t // tm,),
        in_specs=[row(D_HYENA), row(D_DIFF), row(D_FNET),
                  pl.BlockSpec((D_MIX, D_MODEL), lambda i, sid: (0, 0)),
                  row(D_MODEL),
                  pl.BlockSpec((None, 1, 6 * D_MODEL), lambda i, sid: (sid[i], 0, 0)),
                  pl.BlockSpec((1, D_MODEL), lambda i, sid: (0, 0)),
                  pl.BlockSpec((1, D_MODEL), lambda i, sid: (0, 0)),
                  pl.BlockSpec((N_EXPERTS, D_MODEL), lambda i, sid: (0, 0))],
        out_specs=[row(D_MODEL), row(D_MODEL // 2),
                   pl.BlockSpec((N_EXPERTS, tm), lambda i, sid: (0, i))],
    )
    return pl.pallas_call(
        _outproj_kernel,
        grid_spec=grid_spec,
        out_shape=[jax.ShapeDtypeStruct((t, D_MODEL), F32),
                   jax.ShapeDtypeStruct((t, D_MODEL // 2), jnp.uint32),
                   jax.ShapeDtypeStruct((N_EXPERTS, t), F32)],
        compiler_params=_cparams(("arbitrary",)),
        name="out_projection",
    )(sid, hy, at, fn, w_out_bf16, x, mod, ln_g, ln_b, router_wt)


def _route_kernel(lg_ref, bias_ref, tri_ref, eidx_ref, gate_ref, rank_ref, cnt_ref, carry_ref):
    @pl.when(pl.program_id(0) == 0)
    def _():
        carry_ref[...] = jnp.zeros_like(carry_ref)

    gsz = N_EXPERTS // N_GROUPS
    neg = -jnp.inf
    s = 1.0 / (1.0 + jnp.exp(-lg_ref[...]))
    sel = s + bias_ref[...]
    tt = s.shape[1]
    sub = lax.broadcasted_iota(jnp.int32, (gsz, tt), 0)
    gs = []
    for g in range(N_GROUPS):
        blk = sel[g * gsz:(g + 1) * gsz, :]
        m1 = blk.max(0, keepdims=True)
        first = jnp.min(jnp.where(blk == m1, sub, gsz), 0, keepdims=True)
        m2 = jnp.where(sub == first, neg, blk).max(0, keepdims=True)
        gs.append(m1 + m2)
    gscore = jnp.concatenate(gs, 0)
    gi = lax.broadcasted_iota(jnp.int32, (N_GROUPS, tt), 0)
    gkeep = jnp.zeros((N_GROUPS, tt), F32)
    cur = gscore
    for _ in range(TOPK_GROUPS):
        mx = cur.max(0, keepdims=True)
        idx = jnp.min(jnp.where(cur == mx, gi, N_GROUPS), 0, keepdims=True)
        pick = gi == idx
        gkeep = jnp.where(pick, 1.0, gkeep)
        cur = jnp.where(pick, neg, cur)
    keep = jnp.concatenate(
        [jnp.broadcast_to(gkeep[g:g + 1, :], (gsz, tt)) for g in range(N_GROUPS)], 0)
    cur = jnp.where(keep > 0.5, sel, neg)
    ei = lax.broadcasted_iota(jnp.int32, (N_EXPERTS, tt), 0)
    picks, idxs, ws = [], [], []
    onehot = jnp.zeros((N_EXPERTS, tt), F32)
    for _ in range(TOP_K):
        mx = cur.max(0, keepdims=True)
        idx = jnp.min(jnp.where(cur == mx, ei, N_EXPERTS), 0, keepdims=True)
        pick = ei == idx
        picks.append(pick)
        idxs.append(idx)
        ws.append(jnp.sum(jnp.where(pick, s, 0.0), 0, keepdims=True))
        cur = jnp.where(pick, neg, cur)
        onehot = jnp.where(pick, 1.0, onehot)
    wsum = ws[0]
    for w in ws[1:]:
        wsum = wsum + w
    inv = ROUTED_SCALE / wsum
    prefix = jnp.dot(onehot.astype(BF16), tri_ref[...], preferred_element_type=F32)
    rank_e = carry_ref[...] + prefix
    eidx_ref[...] = jnp.concatenate(idxs, 0)
    gate_ref[...] = jnp.concatenate([w * inv for w in ws], 0)
    rank_ref[...] = jnp.concatenate(
        [jnp.sum(jnp.where(pk, rank_e, 0.0), 0, keepdims=True) for pk in picks], 0).astype(jnp.int32)
    carry_ref[...] = carry_ref[...] + jnp.sum(onehot, 1, keepdims=True)
    cnt_ref[...] = carry_ref[...]


def route(logits_t, router_b):
    t = logits_t.shape[1]
    tt = ROUTE_TILE
    tri = jnp.asarray(np.triu(np.ones((tt, tt), np.float32), 1), BF16)
    col = lambda: pl.BlockSpec((TOP_K, tt), lambda i: (0, i))
    return pl.pallas_call(
        _route_kernel,
        grid=(t // tt,),
        in_specs=[pl.BlockSpec((N_EXPERTS, tt), lambda i: (0, i)),
                  pl.BlockSpec((N_EXPERTS, 1), lambda i: (0, 0)),
                  pl.BlockSpec((tt, tt), lambda i: (0, 0))],
        out_specs=[col(), col(), col(), pl.BlockSpec((N_EXPERTS, 1), lambda i: (0, 0))],
        out_shape=[jax.ShapeDtypeStruct((TOP_K, t), jnp.int32),
                   jax.ShapeDtypeStruct((TOP_K, t), F32),
                   jax.ShapeDtypeStruct((TOP_K, t), jnp.int32),
                   jax.ShapeDtypeStruct((N_EXPERTS, 1), F32)],
        scratch_shapes=[pltpu.VMEM((N_EXPERTS, 1), F32)],
        compiler_params=_cparams(("arbitrary",)),
        name="route",
    )(logits_t, router_b.reshape(N_EXPERTS, 1), tri)


def _row_copy(src, dst, sem):
    return pltpu.make_async_copy(src, dst, sem)


def _dispatch_kernel(pos_ref, h_ref, xg_in_ref, xg_ref, sem):
    del xg_in_ref
    tm = h_ref.shape[0]

    def issue(t, carry):
        for k in range(TOP_K):
            p = pos_ref[t * TOP_K + k]
            _row_copy(h_ref.at[pl.ds(t, 1), :], xg_ref.at[pl.ds(p, 1), :], sem).start(priority=k % 2)
        return carry

    lax.fori_loop(0, tm, issue, 0)

    def drain(t, carry):
        for k in range(TOP_K):
            _row_copy(h_ref.at[pl.ds(0, 1), :], xg_ref.at[pl.ds(0, 1), :], sem).wait()
        return carry

    lax.fori_loop(0, tm, drain, 0)


def dispatch(pos_flat, h2, xg):
    t = h2.shape[0]
    tm = ROW_TILE
    return pl.pallas_call(
        _dispatch_kernel,
        grid=(t // tm,),
        in_specs=[pl.BlockSpec((tm * TOP_K,), lambda i: (i,), memory_space=pltpu.SMEM),
                  pl.BlockSpec((tm, D_MODEL // 2), lambda i: (i, 0)),
                  pl.BlockSpec(memory_space=pl.ANY)],
        out_specs=pl.BlockSpec(memory_space=pl.ANY),
        out_shape=jax.ShapeDtypeStruct(xg.shape, xg.dtype),
        scratch_shapes=[pltpu.SemaphoreType.DMA(())],
        input_output_aliases={2: 0},
        compiler_params=_cparams(("arbitrary",)),
        name="moe_dispatch",
    )(pos_flat, h2, xg)


def _expert_kernel(be_ref, nu_ref, x_ref, wg_ref, wu_ref, wd_ref, y_ref):
    del be_ref

    @pl.when(pl.program_id(0) < nu_ref[0])
    def _():
        x = _unpack_bf16_pairs(x_ref[...]).astype(BF16)
        a = (_silu(jnp.dot(x, wg_ref[...], preferred_element_type=F32))
             * jnp.dot(x, wu_ref[...], preferred_element_type=F32))
        y_ref[...] = _pack_bf16_pairs(jnp.dot(a.astype(BF16), wd_ref[...], preferred_element_type=F32))

    @pl.when(pl.program_id(0) >= nu_ref[0])
    def _():
        y_ref[...] = jnp.zeros_like(y_ref)


def expert_ffn(block_exp, n_used, xg, wg, wu, wd):
    p = xg.shape[0]
    blk = MOE_BLK
    grid_spec = pltpu.PrefetchScalarGridSpec(
        num_scalar_prefetch=2,
        grid=(p // blk,),
        in_specs=[pl.BlockSpec((blk, D_MODEL // 2), lambda i, be, nu: (i, 0)),
                  pl.BlockSpec((None, D_MODEL, D_EXPERT), lambda i, be, nu: (be[i], 0, 0)),
                  pl.BlockSpec((None, D_MODEL, D_EXPERT), lambda i, be, nu: (be[i], 0, 0)),
                  pl.BlockSpec((None, D_EXPERT, D_MODEL), lambda i, be, nu: (be[i], 0, 0))],
        out_specs=pl.BlockSpec((blk, D_MODEL // 2), lambda i, be, nu: (i, 0)),
    )
    return pl.pallas_call(
        _expert_kernel,
        grid_spec=grid_spec,
        out_shape=jax.ShapeDtypeStruct((p, D_MODEL // 2), jnp.uint32),
        compiler_params=_cparams(("arbitrary",)),
        name="expert_ffn",
    )(block_exp, n_used, xg, wg, wu, wd)


def _combine_kernel(sid_ref, pos_ref, yb_ref, gate_ref, h_ref, sg_ref, su_ref, sd_ref, x_ref,
                    mod_ref, g_ref, b_ref, o_ref, buf_ref, sem):
    del sid_ref
    tm = h_ref.shape[0]

    def issue(t, carry):
        for k in range(TOP_K):
            p = pos_ref[t * TOP_K + k]
            _row_copy(yb_ref.at[pl.ds(p, 1), :], buf_ref.at[k, pl.ds(t, 1), :], sem).start(priority=k % 2)
        return carry

    lax.fori_loop(0, tm, issue, 0)
    h = _unpack_bf16_pairs(h_ref[...]).astype(BF16)
    a = (_silu(jnp.dot(h, sg_ref[...], preferred_element_type=F32))
         * jnp.dot(h, su_ref[...], preferred_element_type=F32))
    y = jnp.dot(a.astype(BF16), sd_ref[...], preferred_element_type=F32)

    def drain(t, carry):
        for k in range(TOP_K):
            _row_copy(yb_ref.at[pl.ds(0, 1), :], buf_ref.at[0, pl.ds(0, 1), :], sem).wait()
        return carry

    lax.fori_loop(0, tm, drain, 0)
    gates = gate_ref[...]
    for k in range(TOP_K):
        y = y + _unpack_bf16_pairs(buf_ref[k]) * gates[:, k:k + 1]
    g2 = mod_ref[:, 5 * D_MODEL:6 * D_MODEL]
    o_ref[...] = _ln(ALPHA * x_ref[...] + g2 * y) * g_ref[...] + b_ref[...]


def combine(sid, pos_flat, yb, gates_tk, h2, sg, su, sd, x1, mod, ln_g, ln_b):
    t = h2.shape[0]
    tm = COMBINE_TILE
    row = lambda w: pl.BlockSpec((tm, w), lambda i, sid: (i, 0))
    const = lambda shape: pl.BlockSpec(shape, lambda i, sid: (0,) * len(shape))
    grid_spec = pltpu.PrefetchScalarGridSpec(
        num_scalar_prefetch=1,
        grid=(t // tm,),
        in_specs=[pl.BlockSpec((tm * TOP_K,), lambda i, sid: (i,), memory_space=pltpu.SMEM),
                  pl.BlockSpec(memory_space=pl.ANY),
                  row(TOP_K), row(D_MODEL // 2),
                  const((D_MODEL, D_SHARED)), const((D_MODEL, D_SHARED)), const((D_SHARED, D_MODEL)),
                  row(D_MODEL),
                  pl.BlockSpec((None, 1, 6 * D_MODEL), lambda i, sid: (sid[i], 0, 0)),
                  const((1, D_MODEL)), const((1, D_MODEL))],
        out_specs=row(D_MODEL),
        scratch_shapes=[pltpu.VMEM((TOP_K, tm, D_MODEL // 2), jnp.uint32), pltpu.SemaphoreType.DMA(())],
    )
    return pl.pallas_call(
        _combine_kernel,
        grid_spec=grid_spec,
        out_shape=jax.ShapeDtypeStruct((t, D_MODEL), F32),
        compiler_params=_cparams(("arbitrary",)),
        name="moe_combine",
    )(sid, pos_flat, yb, gates_tk, h2, sg, su, sd, x1, mod, ln_g, ln_b)


def _moe_padded_rows(t):
    a = t * TOP_K
    n_blocks = (a + N_EXPERTS * (MOE_BLK - 1) + MOE_BLK - 1) // MOE_BLK
    return n_blocks, n_blocks * MOE_BLK


def moe_slots(eidx, rank, counts, t):
    n_blocks, _ = _moe_padded_rows(t)
    cnt = counts.reshape(-1).astype(jnp.int32)
    padded = (cnt + MOE_BLK - 1) // MOE_BLK * MOE_BLK
    cum_pad = jnp.cumsum(padded)
    pstart = cum_pad - padded
    experts = jnp.arange(N_EXPERTS, dtype=jnp.int32)
    pos = jnp.sum(jnp.where(eidx[None] == experts[:, None, None], pstart[:, None, None], 0), 0) + rank
    starts = jnp.arange(n_blocks, dtype=jnp.int32) * MOE_BLK
    block_exp = jnp.minimum(jnp.sum((cum_pad[None, :] <= starts[:, None]).astype(jnp.int32), 1),
                            N_EXPERTS - 1)
    n_used = (cum_pad[-1] // MOE_BLK).astype(jnp.int32).reshape(1)
    return pos.T.reshape(-1), block_exp, n_used


def _rope_tables():
    n = DIFF_QK // 4
    inv = 1.0 / (ROPE_BASE ** (np.arange(n, dtype=np.float32) / n))
    t = np.arange(SEQ)
    row = (t // GRID_W).astype(np.float32)
    col = (t % GRID_W).astype(np.float32)
    ang = np.stack([row[:, None] * inv, col[:, None] * inv], axis=1).astype(np.float32)
    cos = np.repeat(np.cos(ang).reshape(SEQ, 2 * n), 2, axis=1)
    sin = np.repeat(np.sin(ang).reshape(SEQ, 2 * n), 2, axis=1)
    sign = np.tile(np.array([-1.0, 1.0], np.float32), DIFF_QK // 2)
    sin = sin * sign[None, :]
    cos = np.tile(cos, (1, LANES // DIFF_QK))
    sin = np.tile(sin, (1, LANES // DIFF_QK))
    cos = np.concatenate([cos, np.ones((ROW_TILE, LANES), np.float32)], 0)
    sin = np.concatenate([sin, np.zeros((ROW_TILE, LANES), np.float32)], 0)
    return jnp.asarray(cos, F32), jnp.asarray(sin, F32)


def _tile_ids(t, tile):
    n = t // tile
    i = np.arange(n)
    lat = T_LAT // tile
    sid = np.where(i < lat, i // (SEQ // tile), BATCH).astype(np.int32)
    rid = np.where(i < lat, i % (SEQ // tile), SEQ // tile).astype(np.int32)
    return jnp.asarray(sid), jnp.asarray(rid)


def kernel(x, c, ctx, c_ctx, w_mod, b_mod, w_in, w_out, short_conv_w, filt_w1, filt_b1, filt_w2,
           filt_b2, filt_w3, filt_freq, hyena_bias, lam_q1, lam_k1, lam_q2, lam_k2, subln_w, fnet_w,
           ln1_g, ln1_b, ln2_g, ln2_b, router_w, router_b, exp_w_gate, exp_w_up, exp_w_down,
           sh_w_gate, sh_w_up, sh_w_down):
    xt = jnp.concatenate([x.reshape(T_LAT, D_MODEL), ctx.reshape(T_CTX, D_MODEL)], 0)
    cc = jnp.concatenate([c, c_ctx[None, :], jnp.zeros((3, D_MODEL), F32)], 0)
    cos_t, sin_t = _rope_tables()
    sid_all, rid_all = _tile_ids(T_ALL, ROW_TILE)
    bases = {n: _dft_base(n) for n in (SEQ, CTX_LEN)}
    hy_mats = {n: _hyena_dft_mats(n, bases[n]) for n in (SEQ, CTX_LEN)}
    fn_mats = {n: _fnet_dft_mats(n, bases[n]) for n in (SEQ, CTX_LEN)}
    _, p_rows = _moe_padded_rows(T_ALL)
    xg = jnp.zeros((p_rows, D_MODEL // 2), jnp.uint32)

    for li in range(DEPTH):
        last = li == DEPTH - 1
        lam_init = 0.8 - 0.6 * math.exp(-0.3 * li)
        p = {'short_conv_w': short_conv_w[li], 'filt_w1': filt_w1[li], 'filt_b1': filt_b1[li],
             'filt_w2': filt_w2[li], 'filt_b2': filt_b2[li], 'filt_w3': filt_w3[li],
             'filt_freq': filt_freq[li], 'hyena_bias': hyena_bias[li]}
        mod = small_matmul(cc, w_mod[li], b_mod[li][None, :], silu_in=True, tn=6 * D_MODEL // 4)
        mod = mod.reshape(8, 1, 6 * D_MODEL)
        lam = (jnp.exp(jnp.sum(lam_q1[li] * lam_k1[li])) - jnp.exp(jnp.sum(lam_q2[li] * lam_k2[li]))
               + lam_init).reshape(1).astype(F32)

        hy, q, k, v, fr = in_projection(xt, mod, w_in[li].astype(BF16), cos_t, sin_t, sid_all, rid_all)
        subln2 = jnp.tile(subln_w[li], 2)[None, :]
        t_act = T_LAT if last else T_ALL
        att = diff_attention(lam, q, k, v, subln2, lam_init=lam_init, latent=True)
        hyo = hyena_mixer(hy[:T_LAT].reshape(BATCH, SEQ, HY_W), p, hy_mats[SEQ]).reshape(T_LAT, D_HYENA)
        fno = fourier_mixer(fr[:T_LAT].reshape(BATCH, SEQ, D_FNET), fnet_w[li],
                            fn_mats[SEQ]).reshape(T_LAT, D_FNET)
        if not last:
            att_c = diff_attention(lam, q, k, v, subln2, lam_init=lam_init, latent=False)
            hyo_c = hyena_mixer(hy[T_LAT:].reshape(BATCH, CTX_LEN, HY_W), p, hy_mats[CTX_LEN])
            fno_c = fourier_mixer(fr[T_LAT:].reshape(BATCH, CTX_LEN, D_FNET), fnet_w[li],
                                  fn_mats[CTX_LEN])
            att = jnp.concatenate([att, att_c], 0)
            hyo = jnp.concatenate([hyo, hyo_c.reshape(T_CTX, D_HYENA)], 0)
            fno = jnp.concatenate([fno, fno_c.reshape(T_CTX, D_FNET)], 0)

        sid_r, _ = _tile_ids(t_act, ROW_TILE)
        x1, h2, logits_t = out_projection(
            hyo, att, fno, w_out[li].astype(BF16), xt[:t_act], mod,
            ln1_g[li][None, :], ln1_b[li][None, :], router_w[li].T, sid_r)

        eidx, gates, rank, counts = route(logits_t, router_b[li])
        pos_flat, block_exp, n_used = moe_slots(eidx, rank, counts, T_ALL)
        xg = dispatch(pos_flat, h2, xg)
        yb = expert_ffn(block_exp, n_used, xg, exp_w_gate[li].astype(BF16),
                        exp_w_up[li].astype(BF16), exp_w_down[li].astype(BF16))
        sid_c, _ = _tile_ids(t_act, COMBINE_TILE)
        xt = combine(sid_c, pos_flat, yb, gates.T, h2, sh_w_gate[li].astype(BF16),
                     sh_w_up[li].astype(BF16), sh_w_down[li].astype(BF16), x1, mod,
                     ln2_g[li][None, :], ln2_b[li][None, :])
    return xt[:T_LAT].reshape(BATCH, SEQ, D_MODEL)
```

```python
import functools
import math

import jax
import jax.numpy as jnp
import numpy as np
from jax import lax
from jax.experimental import pallas as pl
from jax.experimental.pallas import tpu as pltpu

F32 = jnp.float32
BF16 = jnp.bfloat16
HIGHEST = lax.Precision.HIGHEST

D_MODEL = 1024
BATCH = 4
SEQ = 4096
DEPTH = 4
GRID_W = 64
CTX_LEN = 256
D_HYENA = 256
HYENA_ORDER = 2
HYENA_EMB = 33
HYENA_FILTER_WIDTH = 64
HYENA_TARGET = 1e-2
HYENA_MIN_DECAY = math.log(HYENA_TARGET) / 1.5
HYENA_MAX_DECAY = math.log(HYENA_TARGET) / 0.3
SHORT_CONV = 3
N_DIFF_HEADS = 8
DIFF_QK = 32
DIFF_V = 64
D_DIFF = N_DIFF_HEADS * DIFF_V
D_FNET = 256
FNET_GROUPS = 4
FNET_GROUP_W = D_FNET // FNET_GROUPS
D_MIX = D_HYENA + D_DIFF + D_FNET
ROPE_BASE = 10000.0
HY_W = 3 * D_HYENA
QK_W = N_DIFF_HEADS * 2 * DIFF_QK
Q_OFF = HY_W
K_OFF = Q_OFF + QK_W
V_OFF = K_OFF + QK_W
F_OFF = V_OFF + D_DIFF
D_IN = F_OFF + D_FNET
N_EXPERTS = 64
TOP_K = 8
N_GROUPS = 8
TOPK_GROUPS = 4
D_EXPERT = 256
D_SHARED = 256
ROUTED_SCALE = 2.5
ALPHA = (2 * DEPTH) ** 0.25
LN_EPS = 1e-5

T_LAT = BATCH * SEQ
T_CTX = BATCH * CTX_LEN
T_ALL = T_LAT + T_CTX

LANES = 128
VMEM_LIMIT = 56 * 1024 * 1024
ROW_TILE = 256
ATTN_TQ = 256
DFT_TILE = 512
ROUTE_TILE = 512
MOE_BLK = 512
COMBINE_TILE = 256
LOG2E = 1.4426950408889634


def _cparams(sem):
    return pltpu.CompilerParams(dimension_semantics=sem, vmem_limit_bytes=VMEM_LIMIT)


def _ln(x):
    mu = jnp.mean(x, -1, keepdims=True)
    xc = x - mu
    var = jnp.mean(xc * xc, -1, keepdims=True)
    return xc * lax.rsqrt(var + LN_EPS)


def _silu(x):
    return x * (1.0 / (1.0 + jnp.exp(-x)))


def _pack_bf16_pairs(x):
    w = x.shape[1] // 2
    lo = pltpu.bitcast(x[:, :w].astype(BF16).astype(F32), jnp.uint32) >> 16
    hi = pltpu.bitcast(x[:, w:].astype(BF16).astype(F32), jnp.uint32) & jnp.uint32(0xFFFF0000)
    return lo | hi


def _unpack_bf16_pairs(p):
    lo = pltpu.bitcast(p << 16, F32)
    hi = pltpu.bitcast(p & jnp.uint32(0xFFFF0000), F32)
    return jnp.concatenate([lo, hi], axis=1)


def _small_mm_kernel(a_ref, w_ref, b_ref, o_ref, *, silu_in):
    a = a_ref[...]
    if silu_in:
        a = _silu(a)
    o_ref[...] = jnp.dot(a, w_ref[...], precision=HIGHEST, preferred_element_type=F32) + b_ref[...]


def small_matmul(a, w, b=None, *, silu_in=False, tn=None):
    m, k = a.shape
    n = w.shape[1]
    tn = n if tn is None else tn
    if b is None:
        b = jnp.zeros((1, n), F32)
    return pl.pallas_call(
        functools.partial(_small_mm_kernel, silu_in=silu_in),
        grid=(n // tn,),
        in_specs=[pl.BlockSpec((m, k), lambda j: (0, 0)),
                  pl.BlockSpec((k, tn), lambda j: (0, j)),
                  pl.BlockSpec((1, tn), lambda j: (0, j))],
        out_specs=pl.BlockSpec((m, tn), lambda j: (0, j)),
        out_shape=jax.ShapeDtypeStruct((m, n), F32),
        compiler_params=_cparams(("arbitrary",)),
        name="small_matmul",
    )(a, w, b)


def _inproj_kernel(sid_ref, rid_ref, x_ref, mod_ref, w_ref, cos_ref, sin_ref,
                   hy_ref, q_ref, k_ref, v_ref, fr_ref):
    del sid_ref, rid_ref
    x = x_ref[...]
    sh = mod_ref[:, 0:D_MODEL]
    sc = mod_ref[:, D_MODEL:2 * D_MODEL]
    h = (_ln(x) * (1.0 + sc) + sh).astype(BF16)
    z = jnp.dot(h, w_ref[...], preferred_element_type=F32)
    hy_ref[...] = z[:, :HY_W]
    cos = jnp.concatenate([cos_ref[...]] * (QK_W // LANES), axis=1)
    sin = jnp.concatenate([sin_ref[...]] * (QK_W // LANES), axis=1)
    lane = lax.broadcasted_iota(jnp.int32, (1, QK_W), 1)
    even = (lane % 2) == 0

    def rope(t):
        nxt = pltpu.roll(t, QK_W - 1, 1)
        prv = pltpu.roll(t, 1, 1)
        return t * cos + jnp.where(even, nxt, prv) * sin

    qscale = (DIFF_QK ** -0.5) * LOG2E
    q_ref[...] = (rope(z[:, Q_OFF:K_OFF]) * qscale).astype(BF16)
    k_ref[...] = rope(z[:, K_OFF:V_OFF]).astype(BF16)
    v_ref[...] = z[:, V_OFF:F_OFF].astype(BF16)
    fr_ref[...] = z[:, F_OFF:].astype(BF16)


def in_projection(x, mod, w_in_bf16, cos_t, sin_t, sid, rid):
    t = x.shape[0]
    tm = ROW_TILE
    grid_spec = pltpu.PrefetchScalarGridSpec(
        num_scalar_prefetch=2,
        grid=(t // tm,),
        in_specs=[
            pl.BlockSpec((tm, D_MODEL), lambda i, sid, rid: (i, 0)),
            pl.BlockSpec((None, 1, 6 * D_MODEL), lambda i, sid, rid: (sid[i], 0, 0)),
            pl.BlockSpec((D_MODEL, D_IN), lambda i, sid, rid: (0, 0)),
            pl.BlockSpec((tm, LANES), lambda i, sid, rid: (rid[i], 0)),
            pl.BlockSpec((tm, LANES), lambda i, sid, rid: (rid[i], 0)),
        ],
        out_specs=[
            pl.BlockSpec((tm, HY_W), lambda i, sid, rid: (i, 0)),
            pl.BlockSpec((tm, QK_W), lambda i, sid, rid: (i, 0)),
            pl.BlockSpec((tm, QK_W), lambda i, sid, rid: (i, 0)),
            pl.BlockSpec((tm, D_DIFF), lambda i, sid, rid: (i, 0)),
            pl.BlockSpec((tm, D_FNET), lambda i, sid, rid: (i, 0)),
        ],
    )
    return pl.pallas_call(
        _inproj_kernel,
        grid_spec=grid_spec,
        out_shape=[jax.ShapeDtypeStruct((t, HY_W), F32),
                   jax.ShapeDtypeStruct((t, QK_W), BF16),
                   jax.ShapeDtypeStruct((t, QK_W), BF16),
                   jax.ShapeDtypeStruct((t, D_DIFF), BF16),
                   jax.ShapeDtypeStruct((t, D_FNET), BF16)],
        compiler_params=_cparams(("arbitrary",)),
        name="in_projection",
    )(sid, rid, x, mod, w_in_bf16, cos_t, sin_t)


def _attn_kernel(lam_ref, q_ref, *refs, lam_init, n_seg):
    k_refs = refs[:n_seg]
    v_refs = refs[n_seg:2 * n_seg]
    w_ref = refs[2 * n_seg]
    o_ref = refs[2 * n_seg + 1]
    q = q_ref[...]
    lane = lax.broadcasted_iota(jnp.int32, (1, LANES), 1)
    lam = lam_ref[0]
    nt = (((1,), (1,)), ((), ()))
    first = lane < DIFF_V
    scores = []
    for u in range(4):
        lo = (u // 2) * DIFF_V + (u % 2) * DIFF_QK
        qm = jnp.where((lane >= lo) & (lane < lo + DIFF_QK), q, jnp.zeros_like(q))
        scores.append([lax.dot_general(qm, kr[...], nt, preferred_element_type=F32) for kr in k_refs])
    probs = []
    for s in scores:
        mx = s[0].max(-1, keepdims=True)
        for sp in s[1:]:
            mx = jnp.maximum(mx, sp.max(-1, keepdims=True))
        e = [jnp.exp2(sp - mx) for sp in s]
        den = e[0].sum(-1, keepdims=True)
        for ep in e[1:]:
            den = den + ep.sum(-1, keepdims=True)
        probs.append((e, 1.0 / den))
    heads = []
    for h in range(2):
        (e0, r0), (e1, r1) = probs[2 * h], probs[2 * h + 1]
        r1 = lam * r1
        o = None
        for seg in range(n_seg):
            w = (e0[seg] * r0 - e1[seg] * r1).astype(BF16)
            part = jnp.dot(w, v_refs[seg][...], preferred_element_type=F32)
            o = part if o is None else o + part
        heads.append(o)
    o = jnp.where(first, heads[0], heads[1])
    sq = o * o
    ms0 = jnp.sum(jnp.where(first, sq, 0.0), -1, keepdims=True)
    ms1 = jnp.sum(jnp.where(first, 0.0, sq), -1, keepdims=True)
    ms = jnp.where(first, ms0, ms1) * (1.0 / DIFF_V)
    y = o * lax.rsqrt(ms + LN_EPS) * w_ref[...] * (1.0 - lam_init)
    o_ref[...] = y.astype(o_ref.dtype)


def diff_attention(lam, q, k, v, subln2, *, lam_init, latent):
    tq = ATTN_TQ
    n_q = (SEQ if latent else CTX_LEN) // tq
    q_base = 0 if latent else T_LAT // tq
    ctx_blk = T_LAT // CTX_LEN

    specs, args = [], []
    for arr in (k, v):
        if latent:
            specs.append(pl.BlockSpec((SEQ, LANES), lambda b, hp, i: (b, hp)))
            args.append(arr)
        specs.append(pl.BlockSpec((CTX_LEN, LANES), lambda b, hp, i: (ctx_blk + b, hp)))
        args.append(arr)
    n_seg = 2 if latent else 1
    rows = T_LAT if latent else T_CTX
    return pl.pallas_call(
        functools.partial(_attn_kernel, lam_init=lam_init, n_seg=n_seg),
        grid=(BATCH, N_DIFF_HEADS // 2, n_q),
        in_specs=[pl.BlockSpec(memory_space=pltpu.SMEM),
                  pl.BlockSpec((tq, LANES), lambda b, hp, i: (q_base + b * n_q + i, hp))] + specs
                 + [pl.BlockSpec((1, LANES), lambda b, hp, i: (0, 0))],
        out_specs=pl.BlockSpec((tq, LANES), lambda b, hp, i: (b * n_q + i, hp)),
        out_shape=jax.ShapeDtypeStruct((rows, D_DIFF), BF16),
        compiler_params=_cparams(("arbitrary", "arbitrary", "arbitrary")),
        name="diff_attention_lat" if latent else "diff_attention_ctx",
    )(lam, q, *args, subln2)


def _short_conv_kernel(u_ref, w_ref, o_ref):
    u = u_ref[...]
    n = u.shape[0]
    row = lax.broadcasted_iota(jnp.int32, (n, 1), 0)
    prv = jnp.where(row == 0, 0.0, pltpu.roll(u, 1, 0))
    nxt = jnp.where(row == n - 1, 0.0, pltpu.roll(u, n - 1, 0))
    o_ref[...] = prv * w_ref[0:1, :] + u * w_ref[1:2, :] + nxt * w_ref[2:3, :]


def short_conv(u, w):
    b, n, _ = u.shape
    return pl.pallas_call(
        _short_conv_kernel,
        grid=(b, 3),
        in_specs=[pl.BlockSpec((None, n, D_HYENA), lambda i, j: (i, 0, j)),
                  pl.BlockSpec((SHORT_CONV, D_HYENA), lambda i, j: (0, j))],
        out_specs=pl.BlockSpec((None, None, n, D_HYENA), lambda i, j: (j, i, 0, 0)),
        out_shape=jax.ShapeDtypeStruct((3, b, n, D_HYENA), F32),
        compiler_params=_cparams(("arbitrary", "arbitrary")),
        name="short_conv",
    )(u, w)


def _filter_kernel(z_ref, w1_ref, b1_ref, w2_ref, b2_ref, w3_ref, fq_ref, dl_ref, h_ref, ny_ref):
    n = z_ref.shape[0]
    fq = fq_ref[...]
    a = jnp.sin(fq * (jnp.dot(z_ref[...], w1_ref[...], precision=HIGHEST,
                              preferred_element_type=F32) + b1_ref[...]))
    a = jnp.sin(fq * (jnp.dot(a, w2_ref[...], precision=HIGHEST,
                              preferred_element_type=F32) + b2_ref[...]))
    row = lax.broadcasted_iota(jnp.int32, (n, 1), 0)
    t = row.astype(F32) * (1.0 / (n - 1))
    win = jnp.exp(-t * dl_ref[...])
    alt = jnp.where(row % 2 == 0, 1.0, -1.0)
    hf = jnp.dot(a, w3_ref[:, 0:D_HYENA], precision=HIGHEST, preferred_element_type=F32) * win
    hb = jnp.dot(a, w3_ref[:, D_HYENA:], precision=HIGHEST, preferred_element_type=F32) * win
    hb0 = jnp.where(row == 0, 0.0, hb)
    l1 = jnp.sum(jnp.abs(hf), 0, keepdims=True) + jnp.sum(jnp.abs(hb0), 0, keepdims=True)
    inv = 1.0 / l1
    hf = hf * inv
    hb0 = hb0 * inv
    ks = hf + hb0
    h_ref[:, 0:D_HYENA] = ks
    h_ref[:, D_HYENA:] = hf - hb0
    ny_ref[...] = jnp.sum(ks * alt, 0, keepdims=True)


def hyena_filters(n, zfeat, w1p, b1, w2, b2, w3, freq, deltas):
    full = lambda shape: pl.BlockSpec(shape, lambda o: (0,) * len(shape))
    return pl.pallas_call(
        _filter_kernel,
        grid=(HYENA_ORDER,),
        in_specs=[full(zfeat.shape), full(w1p.shape), full(b1.shape), full(w2.shape),
                  full(b2.shape),
                  pl.BlockSpec((HYENA_FILTER_WIDTH, 2 * D_HYENA), lambda o: (0, o)),
                  full(freq.shape), full(deltas.shape)],
        out_specs=[pl.BlockSpec((n, 2 * D_HYENA), lambda o: (0, o)),
                   pl.BlockSpec((None, 1, D_HYENA), lambda o: (o, 0, 0))],
        out_shape=[jax.ShapeDtypeStruct((n, 4 * D_HYENA), F32),
                   jax.ShapeDtypeStruct((HYENA_ORDER, 1, D_HYENA), F32)],
        compiler_params=_cparams(("arbitrary",)),
        name="hyena_filters",
    )(zfeat, w1p, b1, w2, b2, w3, freq, deltas)


def _dft_fwd_kernel(c_ref, s_ref, a_ref, b_ref, re_ref, im_ref):
    re_ref[...] = jnp.dot(c_ref[...], a_ref[...].astype(BF16), preferred_element_type=F32)
    im_ref[...] = jnp.dot(s_ref[...], b_ref[...].astype(BF16), preferred_element_type=F32)


def filter_spectrum(cmat, smat, h):
    n = cmat.shape[0]
    tf = min(DFT_TILE, n)
    return pl.pallas_call(
        _dft_fwd_kernel,
        grid=(n // tf, HYENA_ORDER),
        in_specs=[pl.BlockSpec((tf, n), lambda i, o: (i, 0)),
                  pl.BlockSpec((tf, n), lambda i, o: (i, 0)),
                  pl.BlockSpec((n, D_HYENA), lambda i, o: (0, 2 * o)),
                  pl.BlockSpec((n, D_HYENA), lambda i, o: (0, 2 * o + 1))],
        out_specs=[pl.BlockSpec((tf, D_HYENA), lambda i, o: (i, o)),
                   pl.BlockSpec((tf, D_HYENA), lambda i, o: (i, o))],
        out_shape=[jax.ShapeDtypeStruct((n, HYENA_ORDER * D_HYENA), F32)] * 2,
        compiler_params=_cparams(("arbitrary", "arbitrary")),
        name="filter_spectrum",
    )(cmat, smat, h, h)


def _spec_mul_kernel(c_ref, s_ref, z_ref, kre_ref, kim_ref, yre_ref, yim_ref, *, n):
    z = z_ref[...]
    ure = jnp.dot(c_ref[...], z, preferred_element_type=F32)
    uim = jnp.dot(s_ref[...], z, preferred_element_type=F32)
    kre = kre_ref[...]
    kim = kim_ref[...]
    tf = ure.shape[0]
    grow = pl.program_id(0) * tf + lax.broadcasted_iota(jnp.int32, (tf, 1), 0)
    dc = grow == 0
    yre = jnp.where(dc, ure * kre, ure * kre - uim * kim)
    yim = jnp.where(dc, uim * kim, ure * kim + uim * kre)
    wgt = jnp.where(dc, 1.0 / (2 * n), 2.0 / (2 * n))
    yre_ref[...] = (yre * wgt).astype(BF16)
    yim_ref[...] = (yim * wgt).astype(BF16)


def spectral_multiply(cmat, sfwd, z, kre, kim, order):
    b, n, _ = z.shape
    tf = min(DFT_TILE, n)
    return pl.pallas_call(
        functools.partial(_spec_mul_kernel, n=n),
        grid=(n // tf, b),
        in_specs=[pl.BlockSpec((tf, n), lambda i, j: (i, 0)),
                  pl.BlockSpec((tf, n), lambda i, j: (i, 0)),
                  pl.BlockSpec((None, n, D_HYENA), lambda i, j: (j, 0, 0)),
                  pl.BlockSpec((tf, D_HYENA), lambda i, j: (i, order)),
                  pl.BlockSpec((tf, D_HYENA), lambda i, j: (i, order))],
        out_specs=[pl.BlockSpec((None, tf, D_HYENA), lambda i, j: (j, i, 0)),
                   pl.BlockSpec((None, tf, D_HYENA), lambda i, j: (j, i, 0))],
        out_shape=[jax.ShapeDtypeStruct((b, n, D_HYENA), BF16)] * 2,
        compiler_params=_cparams(("arbitrary", "arbitrary")),
        name="spectral_multiply",
    )(cmat, sfwd, z, kre, kim)


def _inv_gate_kernel(c_ref, s_ref, yre_ref, yim_ref, z_ref, g_ref, bias_ref, o_ref):
    y = (jnp.dot(c_ref[...], yre_ref[...], preferred_element_type=F32)
         + jnp.dot(s_ref[...], yim_ref[...], preferred_element_type=F32))
    z = z_ref[...].astype(F32)
    o_ref[...] = (g_ref[...] * (y + z * bias_ref[...])).astype(o_ref.dtype)


def inverse_gate(cmat, sinv, yre, yim, z, gate, bias):
    b, n, _ = z.shape
    tt = min(DFT_TILE, n)
    return pl.pallas_call(
        _inv_gate_kernel,
        grid=(n // tt, b),
        in_specs=[pl.BlockSpec((tt, n), lambda i, j: (i, 0)),
                  pl.BlockSpec((tt, n), lambda i, j: (i, 0)),
                  pl.BlockSpec((None, n, D_HYENA), lambda i, j: (j, 0, 0)),
                  pl.BlockSpec((None, n, D_HYENA), lambda i, j: (j, 0, 0)),
                  pl.BlockSpec((None, tt, D_HYENA), lambda i, j: (j, i, 0)),
                  pl.BlockSpec((None, tt, D_HYENA), lambda i, j: (j, i, 0)),
                  pl.BlockSpec((1, D_HYENA), lambda i, j: (0, 0))],
        out_specs=pl.BlockSpec((None, tt, D_HYENA), lambda i, j: (j, i, 0)),
        out_shape=jax.ShapeDtypeStruct((b, n, D_HYENA), BF16),
        compiler_params=_cparams(("arbitrary", "arbitrary")),
        name="inverse_gate",
    )(cmat, sinv, yre, yim, z, gate, bias)


def _hyena_dft_mats(n, base=None):
    c, s, alt = _dft_base(n) if base is None else base
    f = jnp.arange(n, dtype=jnp.int32)
    sfwd = jnp.where(f[:, None] == 0, alt[None, :], s).astype(BF16)
    return c.astype(BF16), sfwd, sfwd.T


def _dft_base(n):
    big = 2 * n
    f = jnp.arange(n, dtype=jnp.int32)
    m = (f[:, None] * f[None, :]) % big
    ang = m.astype(F32) * (2.0 * math.pi / big)
    alt = jnp.where(f % 2 == 0, 1.0, -1.0).astype(F32)
    c, s = lax.optimization_barrier((jnp.cos(ang), -jnp.sin(ang)))
    return c, s, alt


def _fnet_dft_mats(n, base=None):
    c, s, alt = _dft_base(n) if base is None else base
    ce, se = c[0::2], s[0::2]
    c4 = jnp.concatenate([ce, ce * alt[None, :]], 0)
    s4 = jnp.concatenate([se, se * alt[None, :]], 0)
    return c4.astype(BF16), s4.astype(BF16)


def _hyena_features(n):
    t = np.linspace(0.0, 1.0, n, dtype=np.float32)[:, None]
    bands = (HYENA_EMB - 1) // 2
    omega = (2.0 * math.pi / n) * np.arange(n, dtype=np.float32)[:, None]
    fb = np.linspace(1e-4, bands - 1, bands, dtype=np.float32)[None, :]
    z = np.concatenate([t, np.cos(fb * omega), -np.sin(fb * omega)], -1).astype(np.float32)
    return np.pad(z, ((0, 0), (0, LANES - HYENA_EMB)))


def hyena_mixer(u, p, mats):
    b, n, _ = u.shape
    cmat, sfwd, sinv = mats
    xs = short_conv(u, p['short_conv_w'])
    x1, x2, v = xs[0], xs[1], xs[2]
    deltas = np.abs(np.linspace(HYENA_MIN_DECAY, HYENA_MAX_DECAY, D_HYENA, dtype=np.float32))[None, :]
    w1p = jnp.pad(p['filt_w1'], ((0, LANES - HYENA_EMB), (0, 0)))
    h, nyq = hyena_filters(n, jnp.asarray(_hyena_features(n)), w1p, p['filt_b1'][None, :],
                           p['filt_w2'], p['filt_b2'][None, :], p['filt_w3'],
                           p['filt_freq'][None, :], jnp.asarray(deltas))
    kre, kim = filter_spectrum(cmat, sfwd, h)
    kim = kim.at[0].set(nyq.reshape(-1))
    z = v.astype(BF16)
    for o, gate in enumerate((x1, x2)):
        yre, yim = spectral_multiply(cmat, sfwd, z, kre, kim, o)
        z = inverse_gate(cmat, sinv, yre, yim, z, gate, p['hyena_bias'][o][None, :])
    return z


def _fnet_pq_kernel(x_ref, m_ref, p_ref, q_ref):
    pq = jnp.dot(x_ref[...], m_ref[...].astype(BF16), preferred_element_type=F32)
    p_ref[...] = pq[:, :D_FNET].astype(BF16)
    q_ref[...] = pq[:, D_FNET:].astype(BF16)


def _fnet_dft_kernel(c_ref, s_ref, p_ref, q_ref, o_ref):
    o_ref[...] = (jnp.dot(c_ref[...], p_ref[...], preferred_element_type=F32)
                  + jnp.dot(s_ref[...], q_ref[...], preferred_element_type=F32)).astype(o_ref.dtype)


def fourier_mixer(fr, fnet_w, mats):
    b, n, _ = fr.shape
    cmat, smat = mats
    cw = np.arange(FNET_GROUP_W)
    angw = 2.0 * math.pi * ((cw[:, None] * cw[None, :]) % FNET_GROUP_W) / FNET_GROUP_W
    scale = 1.0 / math.sqrt(n * FNET_GROUP_W)
    eye = np.eye(FNET_GROUPS)
    cbd = np.kron(eye, np.cos(angw) * scale).astype(np.float32)
    sbd = np.kron(eye, np.sin(angw) * scale).astype(np.float32)
    wbd = jnp.zeros((D_FNET, D_FNET), F32)
    for g in range(FNET_GROUPS):
        sl = slice(g * FNET_GROUP_W, (g + 1) * FNET_GROUP_W)
        wbd = wbd.at[sl, sl].set(fnet_w[g])
    mcs = small_matmul(jnp.asarray(np.concatenate([cbd, sbd], 0)), wbd)
    m = jnp.concatenate([mcs[:D_FNET], mcs[D_FNET:]], axis=1)
    tm = min(DFT_TILE, n)
    pm, qm = pl.pallas_call(
        _fnet_pq_kernel,
        grid=(b, n // tm),
        in_specs=[pl.BlockSpec((None, tm, D_FNET), lambda i, j: (i, j, 0)),
                  pl.BlockSpec((D_FNET, 2 * D_FNET), lambda i, j: (0, 0))],
        out_specs=[pl.BlockSpec((None, tm, D_FNET), lambda i, j: (i, j, 0))] * 2,
        out_shape=[jax.ShapeDtypeStruct((b, n, D_FNET), BF16)] * 2,
        compiler_params=_cparams(("arbitrary", "arbitrary")),
        name="fnet_channel_dft",
    )(fr, m)
    return pl.pallas_call(
        _fnet_dft_kernel,
        grid=(n // tm, b),
        in_specs=[pl.BlockSpec((tm, n), lambda i, j: (i, 0)),
                  pl.BlockSpec((tm, n), lambda i, j: (i, 0)),
                  pl.BlockSpec((None, n, D_FNET), lambda i, j: (j, 0, 0)),
                  pl.BlockSpec((None, n, D_FNET), lambda i, j: (j, 0, 0))],
        out_specs=pl.BlockSpec((None, tm, D_FNET), lambda i, j: (j, i, 0)),
        out_shape=jax.ShapeDtypeStruct((b, n, D_FNET), BF16),
        compiler_params=_cparams(("arbitrary", "arbitrary")),
        name="fnet_sequence_dft",
    )(cmat, smat, pm, qm)


def _outproj_kernel(sid_ref, hy_ref, at_ref, fn_ref, w_ref, x_ref, mod_ref, g_ref, b_ref, rw_ref,
                    x1_ref, h2_ref, lg_ref):
    del sid_ref
    mix = (jnp.dot(hy_ref[...], w_ref[0:D_HYENA, :], preferred_element_type=F32)
           + jnp.dot(at_ref[...], w_ref[D_HYENA:D_HYENA + D_DIFF, :], preferred_element_type=F32)
           + jnp.dot(fn_ref[...], w_ref[D_HYENA + D_DIFF:, :], preferred_element_type=F32))
    g1 = mod_ref[:, 2 * D_MODEL:3 * D_MODEL]
    sh2 = mod_ref[:, 3 * D_MODEL:4 * D_MODEL]
    sc2 = mod_ref[:, 4 * D_MODEL:5 * D_MODEL]
    x1 = _ln(ALPHA * x_ref[...] + g1 * mix) * g_ref[...] + b_ref[...]
    x1_ref[...] = x1
    h2 = _ln(x1) * (1.0 + sc2) + sh2
    h2_ref[...] = _pack_bf16_pairs(h2)
    lg_ref[...] = lax.dot_general(rw_ref[...], h2, (((1,), (1,)), ((), ())),
                                  precision=HIGHEST, preferred_element_type=F32)


def out_projection(hy, at, fn, w_out_bf16, x, mod, ln_g, ln_b, router_wt, sid):
    t = x.shape[0]
    tm = ROW_TILE
    row = lambda w: pl.BlockSpec((tm, w), lambda i, sid: (i, 0))
    grid_spec = pltpu.PrefetchScalarGridSpec(
        num_scalar_prefetch=1,
        grid=(t // tm,),
        in_specs=[row(D_HYENA), row(D_DIFF), row(D_FNET),
                  pl.BlockSpec((D_MIX, D_MODEL), lambda i, sid: (0, 0)),
                  row(D_MODEL),
                  pl.BlockSpec((None, 1, 6 * D_MODEL), lambda i, sid: (sid[i], 0, 0)),
                  pl.BlockSpec((1, D_MODEL), lambda i, sid: (0, 0)),
                  pl.BlockSpec((1, D_MODEL), lambda i, sid: (0, 0)),
                  pl.BlockSpec((N_EXPERTS, D_MODEL), lambda i, sid: (0, 0))],
        out_specs=[row(D_MODEL), row(D_MODEL // 2),
                   pl.BlockSpec((N_EXPERTS, tm), lambda i, sid: (0, i))],
    )
    return pl.pallas_call(
        _outproj_kernel,
        grid_spec=grid_spec,
        out_shape=[jax.ShapeDtypeStruct((t, D_MODEL), F32),
                   jax.ShapeDtypeStruct((t, D_MODEL // 2), jnp.uint32),
                   jax.ShapeDtypeStruct((N_EXPERTS, t), F32)],
        compiler_params=_cparams(("arbitrary",)),
        name="out_projection",
    )(sid, hy, at, fn, w_out_bf16, x, mod, ln_g, ln_b, router_wt)


def _route_kernel(lg_ref, bias_ref, tri_ref, eidx_ref, gate_ref, rank_ref, cnt_ref, carry_ref):
    @pl.when(pl.program_id(0) == 0)
    def _():
        carry_ref[...] = jnp.zeros_like(carry_ref)

    gsz = N_EXPERTS // N_GROUPS
    neg = -jnp.inf
    s = 1.0 / (1.0 + jnp.exp(-lg_ref[...]))
    sel = s + bias_ref[...]
    tt = s.shape[1]
    sub = lax.broadcasted_iota(jnp.int32, (gsz, tt), 0)
    gs = []
    for g in range(N_GROUPS):
        blk = sel[g * gsz:(g + 1) * gsz, :]
        m1 = blk.max(0, keepdims=True)
        first = jnp.min(jnp.where(blk == m1, sub, gsz), 0, keepdims=True)
        m2 = jnp.where(sub == first, neg, blk).max(0, keepdims=True)
        gs.append(m1 + m2)
    gscore = jnp.concatenate(gs, 0)
    gi = lax.broadcasted_iota(jnp.int32, (N_GROUPS, tt), 0)
    gkeep = jnp.zeros((N_GROUPS, tt), F32)
    cur = gscore
    for _ in range(TOPK_GROUPS):
        mx = cur.max(0, keepdims=True)
        idx = jnp.min(jnp.where(cur == mx, gi, N_GROUPS), 0, keepdims=True)
        pick = gi == idx
        gkeep = jnp.where(pick, 1.0, gkeep)
        cur = jnp.where(pick, neg, cur)
    keep = jnp.concatenate(
        [jnp.broadcast_to(gkeep[g:g + 1, :], (gsz, tt)) for g in range(N_GROUPS)], 0)
    cur = jnp.where(keep > 0.5, sel, neg)
    ei = lax.broadcasted_iota(jnp.int32, (N_EXPERTS, tt), 0)
    picks, idxs, ws = [], [], []
    onehot = jnp.zeros((N_EXPERTS, tt), F32)
    for _ in range(TOP_K):
        mx = cur.max(0, keepdims=True)
        idx = jnp.min(jnp.where(cur == mx, ei, N_EXPERTS), 0, keepdims=True)
        pick = ei == idx
        picks.append(pick)
        idxs.append(idx)
        ws.append(jnp.sum(jnp.where(pick, s, 0.0), 0, keepdims=True))
        cur = jnp.where(pick, neg, cur)
        onehot = jnp.where(pick, 1.0, onehot)
    wsum = ws[0]
    for w in ws[1:]:
        wsum = wsum + w
    inv = ROUTED_SCALE / wsum
    prefix = jnp.dot(onehot.astype(BF16), tri_ref[...], preferred_element_type=F32)
    rank_e = carry_ref[...] + prefix
    eidx_ref[...] = jnp.concatenate(idxs, 0)
    gate_ref[...] = jnp.concatenate([w * inv for w in ws], 0)
    rank_ref[...] = jnp.concatenate(
        [jnp.sum(jnp.where(pk, rank_e, 0.0), 0, keepdims=True) for pk in picks], 0).astype(jnp.int32)
    carry_ref[...] = carry_ref[...] + jnp.sum(onehot, 1, keepdims=True)
    cnt_ref[...] = carry_ref[...]


def route(logits_t, router_b):
    t = logits_t.shape[1]
    tt = ROUTE_TILE
    tri = jnp.asarray(np.triu(np.ones((tt, tt), np.float32), 1), BF16)
    col = lambda: pl.BlockSpec((TOP_K, tt), lambda i: (0, i))
    return pl.pallas_call(
        _route_kernel,
        grid=(t // tt,),
        in_specs=[pl.BlockSpec((N_EXPERTS, tt), lambda i: (0, i)),
                  pl.BlockSpec((N_EXPERTS, 1), lambda i: (0, 0)),
                  pl.BlockSpec((tt, tt), lambda i: (0, 0))],
        out_specs=[col(), col(), col(), pl.BlockSpec((N_EXPERTS, 1), lambda i: (0, 0))],
        out_shape=[jax.ShapeDtypeStruct((TOP_K, t), jnp.int32),
                   jax.ShapeDtypeStruct((TOP_K, t), F32),
                   jax.ShapeDtypeStruct((TOP_K, t), jnp.int32),
                   jax.ShapeDtypeStruct((N_EXPERTS, 1), F32)],
        scratch_shapes=[pltpu.VMEM((N_EXPERTS, 1), F32)],
        compiler_params=_cparams(("arbitrary",)),
        name="route",
    )(logits_t, router_b.reshape(N_EXPERTS, 1), tri)


def _row_copy(src, dst, sem):
    return pltpu.make_async_copy(src, dst, sem)


def _dispatch_kernel(pos_ref, h_ref, xg_in_ref, xg_ref, sem):
    del xg_in_ref
    tm = h_ref.shape[0]

    def issue(t, carry):
        for k in range(TOP_K):
            p = pos_ref[t * TOP_K + k]
            _row_copy(h_ref.at[pl.ds(t, 1), :], xg_ref.at[pl.ds(p, 1), :], sem).start(priority=k % 2)
        return carry

    lax.fori_loop(0, tm, issue, 0)

    def drain(t, carry):
        for k in range(TOP_K):
            _row_copy(h_ref.at[pl.ds(0, 1), :], xg_ref.at[pl.ds(0, 1), :], sem).wait()
        return carry

    lax.fori_loop(0, tm, drain, 0)


def dispatch(pos_flat, h2, xg):
    t = h2.shape[0]
    tm = ROW_TILE
    return pl.pallas_call(
        _dispatch_kernel,
        grid=(t // tm,),
        in_specs=[pl.BlockSpec((tm * TOP_K,), lambda i: (i,), memory_space=pltpu.SMEM),
                  pl.BlockSpec((tm, D_MODEL // 2), lambda i: (i, 0)),
                  pl.BlockSpec(memory_space=pl.ANY)],
        out_specs=pl.BlockSpec(memory_space=pl.ANY),
        out_shape=jax.ShapeDtypeStruct(xg.shape, xg.dtype),
        scratch_shapes=[pltpu.SemaphoreType.DMA(())],
        input_output_aliases={2: 0},
        compiler_params=_cparams(("arbitrary",)),
        name="moe_dispatch",
    )(pos_flat, h2, xg)


def _expert_kernel(be_ref, nu_ref, x_ref, wg_ref, wu_ref, wd_ref, y_ref):
    del be_ref

    @pl.when(pl.program_id(0) < nu_ref[0])
    def _():
        x = _unpack_bf16_pairs(x_ref[...]).astype(BF16)
        a = (_silu(jnp.dot(x, wg_ref[...], preferred_element_type=F32))
             * jnp.dot(x, wu_ref[...], preferred_element_type=F32))
        y_ref[...] = _pack_bf16_pairs(jnp.dot(a.astype(BF16), wd_ref[...], preferred_element_type=F32))

    @pl.when(pl.program_id(0) >= nu_ref[0])
    def _():
        y_ref[...] = jnp.zeros_like(y_ref)


def expert_ffn(block_exp, n_used, xg, wg, wu, wd):
    p = xg.shape[0]
    blk = MOE_BLK
    grid_spec = pltpu.PrefetchScalarGridSpec(
        num_scalar_prefetch=2,
        grid=(p // blk,),
        in_specs=[pl.BlockSpec((blk, D_MODEL // 2), lambda i, be, nu: (i, 0)),
                  pl.BlockSpec((None, D_MODEL, D_EXPERT), lambda i, be, nu: (be[i], 0, 0)),
                  pl.BlockSpec((None, D_MODEL, D_EXPERT), lambda i, be, nu: (be[i], 0, 0)),
                  pl.BlockSpec((None, D_EXPERT, D_MODEL), lambda i, be, nu: (be[i], 0, 0))],
        out_specs=pl.BlockSpec((blk, D_MODEL // 2), lambda i, be, nu: (i, 0)),
    )
    return pl.pallas_call(
        _expert_kernel,
        grid_spec=grid_spec,
        out_shape=jax.ShapeDtypeStruct((p, D_MODEL // 2), jnp.uint32),
        compiler_params=_cparams(("arbitrary",)),
        name="expert_ffn",
    )(block_exp, n_used, xg, wg, wu, wd)


def _combine_kernel(sid_ref, pos_ref, yb_ref, gate_ref, h_ref, sg_ref, su_ref, sd_ref, x_ref,
                    mod_ref, g_ref, b_ref, o_ref, buf_ref, sem):
    del sid_ref
    tm = h_ref.shape[0]

    def issue(t, carry):
        for k in range(TOP_K):
            p = pos_ref[t * TOP_K + k]
            _row_copy(yb_ref.at[pl.ds(p, 1), :], buf_ref.at[k, pl.ds(t, 1), :], sem).start(priority=k % 2)
        return carry

    lax.fori_loop(0, tm, issue, 0)
    h = _unpack_bf16_pairs(h_ref[...]).astype(BF16)
    a = (_silu(jnp.dot(h, sg_ref[...], preferred_element_type=F32))
         * jnp.dot(h, su_ref[...], preferred_element_type=F32))
    y = jnp.dot(a.astype(BF16), sd_ref[...], preferred_element_type=F32)

    def drain(t, carry):
        for k in range(TOP_K):
            _row_copy(yb_ref.at[pl.ds(0, 1), :], buf_ref.at[0, pl.ds(0, 1), :], sem).wait()
        return carry

    lax.fori_loop(0, tm, drain, 0)
    gates = gate_ref[...]
    for k in range(TOP_K):
        y = y + _unpack_bf16_pairs(buf_ref[k]) * gates[:, k:k + 1]
    g2 = mod_ref[:, 5 * D_MODEL:6 * D_MODEL]
    o_ref[...] = _ln(ALPHA * x_ref[...] + g2 * y) * g_ref[...] + b_ref[...]


def combine(sid, pos_flat, yb, gates_tk, h2, sg, su, sd, x1, mod, ln_g, ln_b):
    t = h2.shape[0]
    tm = COMBINE_TILE
    row = lambda w: pl.BlockSpec((tm, w), lambda i, sid: (i, 0))
    const = lambda shape: pl.BlockSpec(shape, lambda i, sid: (0,) * len(shape))
    grid_spec = pltpu.PrefetchScalarGridSpec(
        num_scalar_prefetch=1,
        grid=(t // tm,),
        in_specs=[pl.BlockSpec((tm * TOP_K,), lambda i, sid: (i,), memory_space=pltpu.SMEM),
                  pl.BlockSpec(memory_space=pl.ANY),
                  row(TOP_K), row(D_MODEL // 2),
                  const((D_MODEL, D_SHARED)), const((D_MODEL, D_SHARED)), const((D_SHARED, D_MODEL)),
                  row(D_MODEL),
                  pl.BlockSpec((None, 1, 6 * D_MODEL), lambda i, sid: (sid[i], 0, 0)),
                  const((1, D_MODEL)), const((1, D_MODEL))],
        out_specs=row(D_MODEL),
        scratch_shapes=[pltpu.VMEM((TOP_K, tm, D_MODEL // 2), jnp.uint32), pltpu.SemaphoreType.DMA(())],
    )
    return pl.pallas_call(
        _combine_kernel,
        grid_spec=grid_spec,
        out_shape=jax.ShapeDtypeStruct((t, D_MODEL), F32),
        compiler_params=_cparams(("arbitrary",)),
        name="moe_combine",
    )(sid, pos_flat, yb, gates_tk, h2, sg, su, sd, x1, mod, ln_g, ln_b)


def _moe_padded_rows(t):
    a = t * TOP_K
    n_blocks = (a + N_EXPERTS * (MOE_BLK - 1) + MOE_BLK - 1) // MOE_BLK
    return n_blocks, n_blocks * MOE_BLK


def moe_slots(eidx, rank, counts, t):
    n_blocks, _ = _moe_padded_rows(t)
    cnt = counts.reshape(-1).astype(jnp.int32)
    padded = (cnt + MOE_BLK - 1) // MOE_BLK * MOE_BLK
    cum_pad = jnp.cumsum(padded)
    pstart = cum_pad - padded
    experts = jnp.arange(N_EXPERTS, dtype=jnp.int32)
    pos = jnp.sum(jnp.where(eidx[None] == experts[:, None, None], pstart[:, None, None], 0), 0) + rank
    starts = jnp.arange(n_blocks, dtype=jnp.int32) * MOE_BLK
    block_exp = jnp.minimum(jnp.sum((cum_pad[None, :] <= starts[:, None]).astype(jnp.int32), 1),
                            N_EXPERTS - 1)
    n_used = (cum_pad[-1] // MOE_BLK).astype(jnp.int32).reshape(1)
    return pos.T.reshape(-1), block_exp, n_used


def _rope_tables():
    n = DIFF_QK // 4
    inv = 1.0 / (ROPE_BASE ** (np.arange(n, dtype=np.float32) / n))
    t = np.arange(SEQ)
    row = (t // GRID_W).astype(np.float32)
    col = (t % GRID_W).astype(np.float32)
    ang = np.stack([row[:, None] * inv, col[:, None] * inv], axis=1).astype(np.float32)
    cos = np.repeat(np.cos(ang).reshape(SEQ, 2 * n), 2, axis=1)
    sin = np.repeat(np.sin(ang).reshape(SEQ, 2 * n), 2, axis=1)
    sign = np.tile(np.array([-1.0, 1.0], np.float32), DIFF_QK // 2)
    sin = sin * sign[None, :]
    cos = np.tile(cos, (1, LANES // DIFF_QK))
    sin = np.tile(sin, (1, LANES // DIFF_QK))
    cos = np.concatenate([cos, np.ones((ROW_TILE, LANES), np.float32)], 0)
    sin = np.concatenate([sin, np.zeros((ROW_TILE, LANES), np.float32)], 0)
    return jnp.asarray(cos, F32), jnp.asarray(sin, F32)


def _tile_ids(t, tile):
    n = t // tile
    i = np.arange(n)
    lat = T_LAT // tile
    sid = np.where(i < lat, i // (SEQ // tile), BATCH).astype(np.int32)
    rid = np.where(i < lat, i % (SEQ // tile), SEQ // tile).astype(np.int32)
    return jnp.asarray(sid), jnp.asarray(rid)


def kernel(x, c, ctx, c_ctx, w_mod, b_mod, w_in, w_out, short_conv_w, filt_w1, filt_b1, filt_w2,
           filt_b2, filt_w3, filt_freq, hyena_bias, lam_q1, lam_k1, lam_q2, lam_k2, subln_w, fnet_w,
           ln1_g, ln1_b, ln2_g, ln2_b, router_w, router_b, exp_w_gate, exp_w_up, exp_w_down,
           sh_w_gate, sh_w_up, sh_w_down):
    xt = jnp.concatenate([x.reshape(T_LAT, D_MODEL), ctx.reshape(T_CTX, D_MODEL)], 0)
    cc = jnp.concatenate([c, c_ctx[None, :], jnp.zeros((3, D_MODEL), F32)], 0)
    cos_t, sin_t = _rope_tables()
    sid_all, rid_all = _tile_ids(T_ALL, ROW_TILE)
    bases = {n: _dft_base(n) for n in (SEQ, CTX_LEN)}
    hy_mats = {n: _hyena_dft_mats(n, bases[n]) for n in (SEQ, CTX_LEN)}
    fn_mats = {n: _fnet_dft_mats(n, bases[n]) for n in (SEQ, CTX_LEN)}
    _, p_rows = _moe_padded_rows(T_ALL)
    xg = jnp.zeros((p_rows, D_MODEL // 2), jnp.uint32)

    for li in range(DEPTH):
        last = li == DEPTH - 1
        lam_init = 0.8 - 0.6 * math.exp(-0.3 * li)
        p = {'short_conv_w': short_conv_w[li], 'filt_w1': filt_w1[li], 'filt_b1': filt_b1[li],
             'filt_w2': filt_w2[li], 'filt_b2': filt_b2[li], 'filt_w3': filt_w3[li],
             'filt_freq': filt_freq[li], 'hyena_bias': hyena_bias[li]}
        mod = small_matmul(cc, w_mod[li], b_mod[li][None, :], silu_in=True, tn=6 * D_MODEL // 4)
        mod = mod.reshape(8, 1, 6 * D_MODEL)
        lam = (jnp.exp(jnp.sum(lam_q1[li] * lam_k1[li])) - jnp.exp(jnp.sum(lam_q2[li] * lam_k2[li]))
               + lam_init).reshape(1).astype(F32)

        hy, q, k, v, fr = in_projection(xt, mod, w_in[li].astype(BF16), cos_t, sin_t, sid_all, rid_all)
        subln2 = jnp.tile(subln_w[li], 2)[None, :]
        t_act = T_LAT if last else T_ALL
        att = diff_attention(lam, q, k, v, subln2, lam_init=lam_init, latent=True)
        hyo = hyena_mixer(hy[:T_LAT].reshape(BATCH, SEQ, HY_W), p, hy_mats[SEQ]).reshape(T_LAT, D_HYENA)
        fno = fourier_mixer(fr[:T_LAT].reshape(BATCH, SEQ, D_FNET), fnet_w[li],
                            fn_mats[SEQ]).reshape(T_LAT, D_FNET)
        if not last:
            att_c = diff_attention(lam, q, k, v, subln2, lam_init=lam_init, latent=False)
            hyo_c = hyena_mixer(hy[T_LAT:].reshape(BATCH, CTX_LEN, HY_W), p, hy_mats[CTX_LEN])
            fno_c = fourier_mixer(fr[T_LAT:].reshape(BATCH, CTX_LEN, D_FNET), fnet_w[li],
                                  fn_mats[CTX_LEN])
            att = jnp.concatenate([att, att_c], 0)
            hyo = jnp.concatenate([hyo, hyo_c.reshape(T_CTX, D_HYENA)], 0)
            fno = jnp.concatenate([fno, fno_c.reshape(T_CTX, D_FNET)], 0)

        sid_r, _ = _tile_ids(t_act, ROW_TILE)
        x1, h2, logits_t = out_projection(
            hyo, att, fno, w_out[li].astype(BF16), xt[:t_act], mod,
            ln1_g[li][None, :], ln1_b[li][None, :], router_w[li].T, sid_r)

        eidx, gates, rank, counts = route(logits_t, router_b[li])
        pos_flat, block_exp, n_used = moe_slots(eidx, rank, counts, T_ALL)
        xg = dispatch(pos_flat, h2, xg)
        yb = expert_ffn(block_exp, n_used, xg, exp_w_gate[li].astype(BF16),
                        exp_w_up[li].astype(BF16), exp_w_down[li].astype(BF16))
        sid_c, _ = _tile_ids(t_act, COMBINE_TILE)
        xt = combine(sid_c, pos_flat, yb, gates.T, h2, sh_w_gate[li].astype(BF16),
                     sh_w_up[li].astype(BF16), sh_w_down[li].astype(BF16), x1, mod,
                     ln2_g[li][None, :], ln2_b[li][None, :])
    return xt[:T_LAT].reshape(BATCH, SEQ, D_MODEL)
```
